```python
import math
import jax, jax.numpy as jnp
from jax import lax
import numpy as np

D_MODEL = 1024
BATCH = 2
SEQ = 16384
DEPTH = 1
DEC_BATCH = 8
DEC_SEQ = 32
PAST_LEN = 4096

CHUNK = 64
POOL_WIDTH = 512
POOL_WINDOWS = (2, 4, 8, 16)
POOL_GROUPS = 4
POOL_GROUP = POOL_WIDTH // POOL_GROUPS
POOL_TAIL = 15
ATTN_HEADS = 4
QK_DIM = 64
V_DIM = 2 * QK_DIM
QK_WIDTH = ATTN_HEADS * 2 * QK_DIM
ATTN_WIDTH = ATTN_HEADS * V_DIM
MIX_WIDTH = POOL_WIDTH + ATTN_WIDTH
IN_WIDTH = POOL_WIDTH + 2 * QK_WIDTH + ATTN_WIDTH
Q_BLOCK = 128
ALIBI_SLOPES = tuple(2.0 ** (-8.0 * (h + 1) / ATTN_HEADS) for h in range(ATTN_HEADS))
PEER_HEADS = 8
PEER_KEYS = 128
PEER_EXPERTS = PEER_KEYS * PEER_KEYS
PEER_HALF = 128
PEER_QDIM = 2 * PEER_HALF
PEER_TOPK = 16
PEER_BLOCK = 256
NORM_EPS = 1e-6
NEG_INF = -1e30

kernel_name = 'hybrid_pool_diffattn_peer_stream_step'


def _rms(x, g):
    xf = x.astype(jnp.float32)
    y = xf * lax.rsqrt(jnp.mean(xf * xf, axis=-1, keepdims=True) + NORM_EPS)
    return (y * g.astype(jnp.float32)).astype(x.dtype)


def _pool_mixer(p, p_prev, offset, w_pool, pool_scale):
    B, T, C = p.shape
    xp = jnp.concatenate([p_prev.astype(p.dtype), p], axis=1).astype(jnp.float32)
    cs = jnp.concatenate([jnp.zeros((B, 1, C), jnp.float32), jnp.cumsum(xp, axis=1)], axis=1)
    pos = offset + jnp.arange(T)
    hi = POOL_TAIL + 1
    means = []
    for g, w in enumerate(POOL_WINDOWS):
        sl = slice(g * POOL_GROUP, (g + 1) * POOL_GROUP)
        win = cs[:, hi:hi + T, sl] - cs[:, hi - w:hi - w + T, sl]
        cnt = jnp.minimum(w, pos + 1).astype(jnp.float32)[None, :, None]
        means.append(win / cnt)
    pooled = (jnp.concatenate(means, axis=-1) - xp[:, POOL_TAIL:]).astype(p.dtype)
    pooled = pooled.reshape(B, T, POOL_GROUPS, POOL_GROUP)
    mixed = jnp.einsum('btgc,gcd->btgd', pooled, w_pool).reshape(B, T, POOL_WIDTH)
    new_tail = jnp.concatenate([p_prev.astype(p.dtype), p], axis=1)[:, -POOL_TAIL:]
    return mixed * pool_scale, new_tail


def _diff_attention(q, K, V, qpos, kpos, lam):
    B, Tq = q.shape[0], q.shape[1]
    scale = QK_DIM ** -0.5
    slopes = jnp.asarray(ALIBI_SLOPES, jnp.float32)
    Kf = K.astype(jnp.float32)
    Vf = V.astype(jnp.float32)
    kchunk = kpos // CHUNK

    def block(args):
        qb, qp = args
        s = jnp.einsum('bqhmd,bshmd->bhmqs', qb.astype(jnp.float32) * scale, Kf)
        dist = jnp.abs(qp[:, None] - kpos[None, :]).astype(jnp.float32)
        bias = -slopes[:, None, None] * dist[None]
        allowed = kchunk[None, :] <= (qp // CHUNK)[:, None]
        s = jnp.where(allowed, s + bias[None, :, None], NEG_INF)
        pr = jax.nn.softmax(s, axis=-1)
        wgt = pr[:, :, 0] - lam * pr[:, :, 1]
        return jnp.einsum('bhqs,bshd->bqhd', wgt, Vf)

    if Tq <= Q_BLOCK:
        return block((q, qpos))
    nb = Tq // Q_BLOCK
    qb = jnp.moveaxis(q.reshape(B, nb, Q_BLOCK, ATTN_HEADS, 2, QK_DIM), 1, 0)
    out = lax.map(block, (qb, qpos.reshape(nb, Q_BLOCK)))
    return jnp.moveaxis(out, 0, 1).reshape(B, Tq, ATTN_HEADS, V_DIM)


def _peer_block(xb, w_q, sub_keys, u_tab, v_tab):
    n = xb.shape[0]
    q = (xb @ w_q).reshape(n, PEER_HEADS, 2, PEER_HALF).astype(jnp.float32)
    s = jnp.einsum('nhpc,hpkc->nhpk', q, sub_keys.astype(jnp.float32))
    sa, ia = lax.top_k(s[:, :, 0], PEER_TOPK)
    sb, ib = lax.top_k(s[:, :, 1], PEER_TOPK)
    cand = (sa[..., :, None] + sb[..., None, :]).reshape(n, PEER_HEADS, PEER_TOPK * PEER_TOPK)
    cidx = (ia[..., :, None] * PEER_KEYS + ib[..., None, :]).reshape(n, PEER_HEADS, PEER_TOPK * PEER_TOPK)
    top_s, sel = lax.top_k(cand, PEER_TOPK)
    eidx = jnp.take_along_axis(cidx, sel, axis=-1)
    gate = jax.nn.softmax(top_s, axis=-1)
    act = jax.nn.gelu(jnp.einsum('nhkd,nd->nhk', u_tab[eidx], xb), approximate=False)
    coef = (gate * act.astype(jnp.float32)).astype(xb.dtype)
    return jnp.einsum('nhk,nhkd->nd', coef, v_tab[eidx])


def _peer(x2d, w_q, sub_keys, u_tab, v_tab):
    n = x2d.shape[0]
    if n <= PEER_BLOCK:
        return _peer_block(x2d, w_q, sub_keys, u_tab, v_tab)
    pad = (-n) % PEER_BLOCK
    xp = jnp.pad(x2d, ((0, pad), (0, 0))).reshape(-1, PEER_BLOCK, x2d.shape[1])
    out = lax.map(lambda xb: _peer_block(xb, w_q, sub_keys, u_tab, v_tab), xp)
    return out.reshape(-1, x2d.shape[1])[:n]


def _layer(x, pool_prev, k_prev, v_prev, lam, lam_init, norm_mix, w_in, q_norm, k_norm, subln,
           w_pool, pool_scale, w_out, norm_ffn, w_peer_q, peer_sub_keys, peer_u, peer_v):
    B, T, D = x.shape
    P = k_prev.shape[1]
    hn = _rms(x, norm_mix)
    proj = hn @ w_in
    p = proj[..., :POOL_WIDTH]
    o = POOL_WIDTH
    q = proj[..., o:o + QK_WIDTH].reshape(B, T, ATTN_HEADS, 2, QK_DIM)
    k = proj[..., o + QK_WIDTH:o + 2 * QK_WIDTH].reshape(B, T, ATTN_HEADS, 2, QK_DIM)
    v = proj[..., o + 2 * QK_WIDTH:].reshape(B, T, ATTN_HEADS, V_DIM)
    q = _rms(q, q_norm)
    k = _rms(k, k_norm)
    k_store = k.reshape(B, T, ATTN_HEADS, 2 * QK_DIM)

    pool_out, pool_tail = _pool_mixer(p, pool_prev, P, w_pool, pool_scale)

    K = jnp.concatenate([k_prev.astype(k_store.dtype), k_store], axis=1)
    K = K.reshape(B, P + T, ATTN_HEADS, 2, QK_DIM)
    V = jnp.concatenate([v_prev.astype(v.dtype), v], axis=1)
    qpos = P + jnp.arange(T)
    kpos = jnp.arange(P + T)
    att = _diff_attention(q, K, V, qpos, kpos, lam)
    att = (_rms(att, subln) * (1.0 - lam_init)).astype(x.dtype).reshape(B, T, ATTN_WIDTH)

    mix = jnp.concatenate([pool_out.astype(x.dtype), att], axis=-1)
    h = x + mix @ w_out
    ffn = _peer(_rms(h, norm_ffn).reshape(B * T, D), w_peer_q, peer_sub_keys, peer_u, peer_v)
    y = h + ffn.reshape(B, T, D)
    return y, k_store, v, pool_tail


def setup_inputs(seed: int = 0) -> dict:
    key = jax.random.key(seed)
    ks = jax.random.split(key, 24)
    f32 = jnp.float32
    nrm = lambda k, shape, s: jax.random.normal(k, shape, f32) * s
    return {
        'x_prompt': nrm(ks[0], (BATCH, SEQ, D_MODEL), 1.0),
        'x_sample': nrm(ks[1], (DEC_BATCH, DEC_SEQ, D_MODEL), 1.0),
        'cache_k': nrm(ks[2], (DEPTH, DEC_BATCH, PAST_LEN, ATTN_HEADS, 2 * QK_DIM), 1.0),
        'cache_v': nrm(ks[3], (DEPTH, DEC_BATCH, PAST_LEN, ATTN_HEADS, V_DIM), 1.0),
        'state_pool': nrm(ks[4], (DEPTH, DEC_BATCH, POOL_TAIL, POOL_WIDTH), 1.0),
        'norm_mix': 1.0 + nrm(ks[5], (DEPTH, D_MODEL), 0.02),
        'w_in': nrm(ks[6], (DEPTH, D_MODEL, IN_WIDTH), D_MODEL ** -0.5),
        'q_norm': 1.0 + nrm(ks[7], (DEPTH, QK_DIM), 0.02),
        'k_norm': 1.0 + nrm(ks[8], (DEPTH, QK_DIM), 0.02),
        'lambda_q1': nrm(ks[9], (DEPTH, QK_DIM), 0.1),
        'lambda_k1': nrm(ks[10], (DEPTH, QK_DIM), 0.1),
        'lambda_q2': nrm(ks[11], (DEPTH, QK_DIM), 0.1),
        'lambda_k2': nrm(ks[12], (DEPTH, QK_DIM), 0.1),
        'subln': 1.0 + nrm(ks[13], (DEPTH, V_DIM), 0.02),
        'w_pool': nrm(ks[14], (DEPTH, POOL_GROUPS, POOL_GROUP, POOL_GROUP), POOL_GROUP ** -0.5),
        'pool_scale': 1.0 + nrm(ks[15], (DEPTH, POOL_WIDTH), 0.1),
        'w_out': nrm(ks[16], (DEPTH, MIX_WIDTH, D_MODEL), MIX_WIDTH ** -0.5),
        'norm_ffn': 1.0 + nrm(ks[17], (DEPTH, D_MODEL), 0.02),
        'w_peer_q': nrm(ks[18], (DEPTH, D_MODEL, PEER_HEADS * PEER_QDIM), D_MODEL ** -0.5),
        'peer_sub_keys': nrm(ks[19], (DEPTH, PEER_HEADS, 2, PEER_KEYS, PEER_HALF), PEER_HALF ** -0.5),
        'peer_u': nrm(ks[20], (DEPTH, PEER_EXPERTS, D_MODEL), D_MODEL ** -0.5),
        'peer_v': nrm(ks[21], (DEPTH, PEER_EXPERTS, D_MODEL), PEER_HEADS ** -0.5),
    }


def reference(x_prompt, x_sample, cache_k, cache_v, state_pool, norm_mix, w_in, q_norm, k_norm,
              lambda_q1, lambda_k1, lambda_q2, lambda_k2, subln, w_pool, pool_scale, w_out,
              norm_ffn, w_peer_q, peer_sub_keys, peer_u, peer_v):
    f32 = jnp.float32
    yp, ys = x_prompt, x_sample
    Bp = x_prompt.shape[0]
    kp_l, vp_l, pp_l, ks_l, vs_l, ps_l = [], [], [], [], [], []
    for l in range(DEPTH):
        lam_init = 0.8 - 0.6 * math.exp(-0.3 * l)
        lam = (jnp.exp(jnp.sum(lambda_q1[l].astype(f32) * lambda_k1[l].astype(f32)))
               - jnp.exp(jnp.sum(lambda_q2[l].astype(f32) * lambda_k2[l].astype(f32))) + lam_init)
        params = (norm_mix[l], w_in[l], q_norm[l], k_norm[l], subln[l], w_pool[l], pool_scale[l],
                  w_out[l], norm_ffn[l], w_peer_q[l], peer_sub_keys[l], peer_u[l], peer_v[l])
        zero_pool = jnp.zeros((Bp, POOL_TAIL, POOL_WIDTH), yp.dtype)
        zero_k = jnp.zeros((Bp, 0, ATTN_HEADS, 2 * QK_DIM), yp.dtype)
        zero_v = jnp.zeros((Bp, 0, ATTN_HEADS, V_DIM), yp.dtype)
        yp, kp, vp, pp = _layer(yp, zero_pool, zero_k, zero_v, lam, lam_init, *params)
        ys, kn, vn, pn = _layer(ys, state_pool[l], cache_k[l], cache_v[l], lam, lam_init, *params)
        kp_l.append(kp); vp_l.append(vp); pp_l.append(pp)
        ks_l.append(kn); vs_l.append(vn); ps_l.append(pn)
    return (yp, ys, jnp.stack(kp_l), jnp.stack(vp_l), jnp.stack(pp_l),
            jnp.stack(ks_l), jnp.stack(vs_l), jnp.stack(ps_l))
```

```python
import functools
import math

import jax
import jax.numpy as jnp
from jax import lax
from jax.experimental import pallas as pl
from jax.experimental.pallas import tpu as pltpu

F32 = jnp.float32
BF16 = jnp.bfloat16

D_MODEL = 1024
CHUNK = 64
POOL_WIDTH = 512
POOL_WINDOWS = (2, 4, 8, 16)
POOL_GROUP = 128
POOL_TAIL = 15
POOL_HALO = 16
ATTN_HEADS = 4
QK_DIM = 64
V_DIM = 128
QK_WIDTH = 512
ATTN_WIDTH = 512
IN_WIDTH = 2048
ALIBI_SLOPES = tuple(2.0 ** (-8.0 * (h + 1) / ATTN_HEADS) for h in range(ATTN_HEADS))
PEER_HEADS = 8
PEER_KEYS = 128
PEER_EXPERTS = PEER_KEYS * PEER_KEYS
PEER_HALF = 128
PEER_TOPK = 16
NORM_EPS = 1e-6
NEG_INF = -1e30
LAM_INIT = 0.8 - 0.6 * math.exp(-0.3 * 0)

VMEM_LIMIT_BYTES = 56 * 1024 * 1024

_CAND_PIECES = ((0, 1, 16), (1, 1, 8), (2, 1, 5), (3, 1, 4), (4, 1, 3), (5, 1, 2), (6, 1, 2), (7, 1, 2), (8, 8, 1))
_CAND_ROWS = 80


def _params(semantics):
    return pltpu.CompilerParams(dimension_semantics=semantics, vmem_limit_bytes=VMEM_LIMIT_BYTES)


def _rms_rows(x, gain):
    ms = jnp.mean(x * x, axis=-1, keepdims=True)
    return x * lax.rsqrt(ms + NORM_EPS) * gain


def _inproj_kernel(x_ref, g_ref, win_ref, qg_ref, kg_ref, gm_ref, pprev_ref, wpool_ref, pscale_ref,
                   p_ref, ks_ref, v_ref, qb_ref, kb_ref, vb_ref, pool_ref, ext_ref,
                   *, tm, tiles_per_batch, pos0):
    t_in_batch = lax.rem(pl.program_id(0), tiles_per_batch)
    hn = _rms_rows(x_ref[...], g_ref[...]).astype(BF16)
    proj = jnp.dot(hn, win_ref[...], preferred_element_type=F32)
    p = proj[:, :POOL_WIDTH]
    q = proj[:, POOL_WIDTH:POOL_WIDTH + QK_WIDTH]
    k = proj[:, POOL_WIDTH + QK_WIDTH:POOL_WIDTH + 2 * QK_WIDTH]
    v = proj[:, POOL_WIDTH + 2 * QK_WIDTH:]

    def group_rms(z, gain):
        sq = z * z
        hi = sq.astype(BF16)
        lo = (sq - hi.astype(F32)).astype(BF16)
        ss = (jnp.dot(hi, gm_ref[...], preferred_element_type=F32)
              + jnp.dot(lo, gm_ref[...], preferred_element_type=F32))
        return z * lax.rsqrt(ss * (1.0 / QK_DIM) + NORM_EPS) * gain

    qn = group_rms(q, qg_ref[...])
    kn = group_rms(k, kg_ref[...])
    p_ref[...] = p
    ks_ref[...] = kn
    v_ref[...] = v
    qb_ref[...] = (qn * (QK_DIM ** -0.5)).astype(BF16)
    kb_ref[...] = kn.astype(BF16)
    vb_ref[...] = v.astype(BF16)

    @pl.when(t_in_batch == 0)
    def _():
        ext_ref[0:POOL_HALO, :] = pprev_ref[0]

    ext_ref[POOL_HALO:POOL_HALO + tm, :] = p
    pos = lax.broadcasted_iota(jnp.int32, (tm, POOL_GROUP), 0) + (pos0 + t_in_batch * tm)
    mixed = []
    for g, w in enumerate(POOL_WINDOWS):
        lanes = slice(g * POOL_GROUP, (g + 1) * POOL_GROUP)
        win = ext_ref[POOL_HALO:POOL_HALO + tm, lanes]
        for j in range(1, w):
            win = win + ext_ref[POOL_HALO - j:POOL_HALO - j + tm, lanes]
        cnt = jnp.minimum(w, pos + 1).astype(F32)
        pooled = win / cnt - p[:, lanes]
        mixed.append(jnp.dot(pooled.astype(BF16), wpool_ref[g], preferred_element_type=F32))
    pool_ref[...] = (jnp.concatenate(mixed, axis=1) * pscale_ref[...]).astype(BF16)
    ext_ref[0:POOL_HALO, :] = ext_ref[tm:tm + POOL_HALO, :]


def _inproj(x2d, pprev, norm_mix, w_in_bf, qg, kg, gm, w_pool_bf, pool_scale, *, seq, pos0):
    n = x2d.shape[0]
    tm = min(seq, 512)
    tiles_per_batch = seq // tm
    row = lambda i: (i, 0)
    full2 = lambda i: (0, 0)
    out_w = lambda dt: jax.ShapeDtypeStruct((n, POOL_WIDTH), dt)
    return pl.pallas_call(
        functools.partial(_inproj_kernel, tm=tm, tiles_per_batch=tiles_per_batch, pos0=pos0),
        grid=(n // tm,),
        in_specs=[
            pl.BlockSpec((tm, D_MODEL), row),
            pl.BlockSpec((1, D_MODEL), full2),
            pl.BlockSpec((D_MODEL, IN_WIDTH), full2),
            pl.BlockSpec((1, QK_WIDTH), full2),
            pl.BlockSpec((1, QK_WIDTH), full2),
            pl.BlockSpec((QK_WIDTH, QK_WIDTH), full2),
            pl.BlockSpec((1, POOL_HALO, POOL_WIDTH), lambda i: (i // tiles_per_batch, 0, 0)),
            pl.BlockSpec((len(POOL_WINDOWS), POOL_GROUP, POOL_GROUP), lambda i: (0, 0, 0)),
            pl.BlockSpec((1, POOL_WIDTH), full2),
        ],
        out_specs=[pl.BlockSpec((tm, POOL_WIDTH), row)] * 7,
        out_shape=[out_w(F32), out_w(F32), out_w(F32), out_w(BF16), out_w(BF16), out_w(BF16), out_w(BF16)],
        scratch_shapes=[pltpu.VMEM((tm + POOL_HALO, POOL_WIDTH), F32)],
        compiler_params=_params(("arbitrary",)),
        name="inproj_pool",
    )(x2d, norm_mix, w_in_bf, qg, kg, gm, pprev, w_pool_bf, pool_scale)


def _attn_kernel(slopes_ref, q_ref, k_ref, v_ref, lam_ref, subln_ref, o_ref, boff_ref, bdiag_ref,
                 *, tq, tk, pos0):
    h = pl.program_id(1)
    qi = pl.program_id(2)
    slope = slopes_ref[h]

    @pl.when(qi == 0)
    def _():
        r = lax.rem(lax.broadcasted_iota(jnp.int32, (2 * tq, tk), 0), tq)
        c = lax.broadcasted_iota(jnp.int32, (2 * tq, tk), 1)
        boff_ref[...] = -slope * (r - c).astype(F32)
        r = lax.rem(lax.broadcasted_iota(jnp.int32, (2 * tq, tq), 0), tq)
        c = lax.broadcasted_iota(jnp.int32, (2 * tq, tq), 1)
        allowed = (c // CHUNK) <= (r // CHUNK)
        bdiag_ref[...] = jnp.where(allowed, -slope * jnp.abs(r - c).astype(F32), NEG_INF)

    q = q_ref[0]
    lane = lax.broadcasted_iota(jnp.int32, q.shape, 1)
    zero = jnp.zeros_like(q)
    qs = jnp.concatenate([jnp.where(lane < QK_DIM, q, zero), jnp.where(lane >= QK_DIM, q, zero)], axis=0)
    base = pos0 + qi * tq

    def update(s, vblk, carry):
        m, l, acc = carry
        m_new = jnp.maximum(m, jnp.max(s, axis=1, keepdims=True))
        alpha = jnp.exp(m - m_new)
        p = jnp.exp(s - m_new)
        l = alpha * l + jnp.sum(p, axis=1, keepdims=True)
        acc = alpha * acc + jnp.dot(p.astype(BF16), vblk, preferred_element_type=F32)
        return m_new, l, acc

    def scores(kblk):
        return lax.dot_general(qs, kblk, (((1,), (1,)), ((), ())), preferred_element_type=F32)

    def past_block(kb, carry):
        start = pl.multiple_of(kb * tk, tk)
        s = scores(k_ref[0, pl.ds(start, tk), :])
        s = s + boff_ref[...] - slope * (base - kb * tk).astype(F32)
        return update(s, v_ref[0, pl.ds(start, tk), :], carry)

    carry = (jnp.full((2 * tq, 1), NEG_INF, F32), jnp.zeros((2 * tq, 1), F32), jnp.zeros((2 * tq, V_DIM), F32))
    carry = lax.fori_loop(0, base // tk, past_block, carry)
    dstart = pl.multiple_of(base, tq)
    s = scores(k_ref[0, pl.ds(dstart, tq), :]) + bdiag_ref[...]
    _, l, acc = update(s, v_ref[0, pl.ds(dstart, tq), :], carry)

    lam_v = lam_ref[...]
    lam = (jnp.exp(jnp.sum(lam_v[0:1] * lam_v[1:2], axis=1, keepdims=True))
           - jnp.exp(jnp.sum(lam_v[2:3] * lam_v[3:4], axis=1, keepdims=True)) + LAM_INIT)
    o = acc / l
    o = o[:tq] - lam * o[tq:]
    o_ref[0] = (_rms_rows(o, subln_ref[...]) * (1.0 - LAM_INIT)).astype(BF16)


def _attention(q_bf, k_all, v_all, lam_vecs, subln, *, pos0):
    b, t, _ = q_bf.shape
    s = k_all.shape[1]
    tq = min(t, 256)
    tk = 256
    assert pos0 % tk == 0 and t % tq == 0 and (tq == tk or t == tq) and pos0 % CHUNK == 0 and tq % 8 == 0
    assert s == pos0 + t
    slopes = jnp.asarray(ALIBI_SLOPES, F32)
    return pl.pallas_call(
        functools.partial(_attn_kernel, tq=tq, tk=tk, pos0=pos0),
        grid=(b, ATTN_HEADS, t // tq),
        in_specs=[
            pl.BlockSpec(memory_space=pltpu.SMEM),
            pl.BlockSpec((1, tq, V_DIM), lambda bi, hi, qi: (bi, qi, hi)),
            pl.BlockSpec((1, s, V_DIM), lambda bi, hi, qi: (bi, 0, hi)),
            pl.BlockSpec((1, s, V_DIM), lambda bi, hi, qi: (bi, 0, hi)),
            pl.BlockSpec((4, QK_DIM), lambda bi, hi, qi: (0, 0)),
            pl.BlockSpec((1, V_DIM), lambda bi, hi, qi: (0, 0)),
        ],
        out_specs=pl.BlockSpec((1, tq, V_DIM), lambda bi, hi, qi: (bi, qi, hi)),
        out_shape=jax.ShapeDtypeStruct((b, t, ATTN_WIDTH), BF16),
        scratch_shapes=[pltpu.VMEM((2 * tq, tk), F32), pltpu.VMEM((2 * tq, tq), F32)],
        compiler_params=_params(("arbitrary", "arbitrary", "arbitrary")),
        name="diff_attention",
    )(slopes, q_bf, k_all, v_all, lam_vecs, subln)


def _outproj_kernel(x_ref, pool_ref, att_ref, wout_ref, g_ref, wq_ref, sk_ref, h_ref, hnt_ref, st_ref):
    mix = (jnp.dot(pool_ref[...], wout_ref[0:POOL_WIDTH, :], preferred_element_type=F32)
           + jnp.dot(att_ref[...], wout_ref[POOL_WIDTH:, :], preferred_element_type=F32))
    h = x_ref[...] + mix
    h_ref[...] = h
    hn = _rms_rows(h, g_ref[...])
    hnt_ref[...] = hn.T.astype(BF16)
    q = jnp.dot(hn.astype(BF16), wq_ref[...], preferred_element_type=F32)
    for g in range(2 * PEER_HEADS):
        qg = q[:, g * PEER_HALF:(g + 1) * PEER_HALF].astype(BF16)
        st_ref[g] = lax.dot_general(sk_ref[g], qg, (((1,), (1,)), ((), ())), preferred_element_type=F32)


def _outproj(x2d, pool_bf, att_bf, w_out_bf, norm_ffn, w_q_bf, sub_keys_bf):
    n = x2d.shape[0]
    tm = min(n, 256)
    row = lambda i: (i, 0)
    full2 = lambda i: (0, 0)
    return pl.pallas_call(
        _outproj_kernel,
        grid=(n // tm,),
        in_specs=[
            pl.BlockSpec((tm, D_MODEL), row),
            pl.BlockSpec((tm, POOL_WIDTH), row),
            pl.BlockSpec((tm, ATTN_WIDTH), row),
            pl.BlockSpec((D_MODEL, D_MODEL), full2),
            pl.BlockSpec((1, D_MODEL), full2),
            pl.BlockSpec((D_MODEL, 2 * PEER_HEADS * PEER_HALF), full2),
            pl.BlockSpec((2 * PEER_HEADS, PEER_KEYS, PEER_HALF), lambda i: (0, 0, 0)),
        ],
        out_specs=[
            pl.BlockSpec((tm, D_MODEL), row),
            pl.BlockSpec((D_MODEL, tm), lambda i: (0, i)),
            pl.BlockSpec((2 * PEER_HEADS, PEER_KEYS, tm), lambda i: (0, 0, i)),
        ],
        out_shape=[
            jax.ShapeDtypeStruct((n, D_MODEL), F32),
            jax.ShapeDtypeStruct((D_MODEL, n), BF16),
            jax.ShapeDtypeStruct((2 * PEER_HEADS, PEER_KEYS, n), F32),
        ],
        compiler_params=_params(("arbitrary",)),
        name="outproj_peer_query",
    )(x2d, pool_bf, att_bf, w_out_bf, norm_ffn, w_q_bf, sub_keys_bf)


def _top16_rows(s, iota_k):
    tn = s.shape[1]
    row16 = lax.broadcasted_iota(jnp.int32, (PEER_TOPK, tn), 0)
    rank = jnp.full(s.shape, float(PEER_TOPK), F32)
    vals = jnp.zeros((PEER_TOPK, tn), F32)
    cur = s
    for i in range(PEER_TOPK):
        m = jnp.max(cur, axis=0, keepdims=True)
        idx = jnp.min(jnp.where(cur == m, iota_k, s.shape[0]), axis=0, keepdims=True)
        hit = iota_k == idx
        rank = jnp.where(hit, float(i), rank)
        cur = jnp.where(hit, NEG_INF, cur)
        vals = jnp.where(row16 == i, m, vals)
    return vals, rank


def _topk_kernel(s_ref, cnt_ref, ea_ref, rank_ref, eb_ref, *, tn):
    iota_k = lax.broadcasted_iota(jnp.int32, (PEER_KEYS, tn), 0)
    iota_c = lax.broadcasted_iota(jnp.int32, (_CAND_ROWS, tn), 0)
    row8 = lax.broadcasted_iota(jnp.int32, (8, tn), 0)
    for h in range(PEER_HEADS):
        sa = s_ref[2 * h]
        sb = s_ref[2 * h + 1]
        va, rank_a = _top16_rows(sa, iota_k)
        vb, rank_b = _top16_rows(sb, iota_k)
        pieces = []
        for i0, ni, nj in _CAND_PIECES:
            if ni == 1:
                rows = 16 if nj > 8 else 8
                piece = va[i0:i0 + 1] + vb[0:rows]
                if nj < rows:
                    piece = jnp.where(row8 < nj, piece, NEG_INF)
            else:
                piece = va[i0:i0 + ni] + vb[0:1]
            pieces.append(piece)
        cand = jnp.concatenate(pieces, axis=0)
        cur = cand
        sel = jnp.zeros(cand.shape, F32)
        for _ in range(PEER_TOPK):
            m = jnp.max(cur, axis=0, keepdims=True)
            idx = jnp.min(jnp.where(cur == m, iota_c, _CAND_ROWS), axis=0, keepdims=True)
            hit = iota_c == idx
            sel = jnp.where(hit, 1.0, sel)
            cur = jnp.where(hit, NEG_INF, cur)
        z = jnp.sum(jnp.where(sel > 0.0, jnp.exp(cand - cand[0:1]), 0.0), axis=0, keepdims=True)
        cnt = jnp.zeros(sa.shape, F32)
        off = 0
        for i0, ni, nj in _CAND_PIECES:
            rows = (16 if nj > 8 else 8) if ni == 1 else ni
            if ni == 1:
                c_i = jnp.sum(sel[off:off + rows], axis=0, keepdims=True)
                cnt = jnp.where(rank_a == float(i0), c_i, cnt)
            else:
                for r in range(ni):
                    cnt = jnp.where(rank_a == float(i0 + r), sel[off + r:off + r + 1], cnt)
            off += rows
        cnt_ref[h] = cnt
        ea_ref[h] = jnp.exp(sa - va[0:1])
        rank_ref[h] = rank_b
        eb_ref[h] = jnp.exp(sb - vb[0:1]) / z


def _topk(scores_t):
    n = scores_t.shape[2]
    tn = min(n, 128)
    spec_in = pl.BlockSpec((2 * PEER_HEADS, PEER_KEYS, tn), lambda i: (0, 0, i))
    spec_out = pl.BlockSpec((PEER_HEADS, PEER_KEYS, tn), lambda i: (0, 0, i))
    shape_out = jax.ShapeDtypeStruct((PEER_HEADS, PEER_KEYS, n), F32)
    return pl.pallas_call(
        functools.partial(_topk_kernel, tn=tn),
        grid=(n // tn,),
        in_specs=[spec_in],
        out_specs=[spec_out] * 4,
        out_shape=[shape_out] * 4,
        compiler_params=_params(("arbitrary",)),
        name="peer_topk",
    )(scores_t)


def _peer_kernel(hnt_ref, h_ref, u_ref, vt_ref, cnt_ref, ea_ref, rank_ref, eb_ref, y_ref,
                 acc_ref, act_ref, coef_ref, *, te):
    e = pl.program_id(1)
    blocks = te // PEER_KEYS

    @pl.when(e == 0)
    def _():
        acc_ref[...] = jnp.zeros_like(acc_ref)

    act_ref[...] = jnp.dot(u_ref[...], hnt_ref[...], preferred_element_type=F32)

    def a_block(a, _):
        a_glob = e * blocks + a
        rows = pl.ds(pl.multiple_of(a * PEER_KEYS, PEER_KEYS), PEER_KEYS)
        gate = jnp.zeros((PEER_KEYS, act_ref.shape[1]), F32)
        for h in range(PEER_HEADS):
            cnt = cnt_ref[h, pl.ds(a_glob, 1), :]
            ea = ea_ref[h, pl.ds(a_glob, 1), :]
            gate = gate + jnp.where(rank_ref[h] < cnt, eb_ref[h], 0.0) * ea
        x = act_ref[rows, :]
        gelu = 0.5 * x * (1.0 + lax.erf(x * (2.0 ** -0.5)))
        coef_ref[rows, :] = (gate * gelu).astype(BF16)
        return 0

    lax.fori_loop(0, blocks, a_block, 0)
    acc_ref[...] += jnp.dot(vt_ref[...], coef_ref[...], preferred_element_type=F32)

    @pl.when(e == pl.num_programs(1) - 1)
    def _():
        y_ref[...] = h_ref[...] + acc_ref[...].T


def _peer(hn_t, h2d, u_bf, vt_bf, cnt, ea, rank, eb):
    n = h2d.shape[0]
    tn = min(n, 512)
    te = 1024
    tok = lambda i, e: (0, 0, i)
    fac = pl.BlockSpec((PEER_HEADS, PEER_KEYS, tn), tok)
    return pl.pallas_call(
        functools.partial(_peer_kernel, te=te),
        grid=(n // tn, PEER_EXPERTS // te),
        in_specs=[
            pl.BlockSpec((D_MODEL, tn), lambda i, e: (0, i)),
            pl.BlockSpec((tn, D_MODEL), lambda i, e: (i, 0)),
            pl.BlockSpec((te, D_MODEL), lambda i, e: (e, 0)),
            pl.BlockSpec((D_MODEL, te), lambda i, e: (0, e)),
            fac, fac, fac, fac,
        ],
        out_specs=pl.BlockSpec((tn, D_MODEL), lambda i, e: (i, 0)),
        out_shape=jax.ShapeDtypeStruct((n, D_MODEL), F32),
        scratch_shapes=[pltpu.VMEM((D_MODEL, tn), F32), pltpu.VMEM((te, tn), F32), pltpu.VMEM((te, tn), BF16)],
        compiler_params=_params(("arbitrary", "arbitrary")),
        name="peer_dense",
    )(hn_t, h2d, u_bf, vt_bf, cnt, ea, rank, eb)


def _stream(x, pool_prev, k_prev, v_prev, w, lam_vecs):
    b, t, _ = x.shape
    past = k_prev.shape[1]
    n = b * t
    x2d = x.reshape(n, D_MODEL)
    pprev = jnp.concatenate([jnp.zeros((b, POOL_HALO - POOL_TAIL, POOL_WIDTH), F32), pool_prev], axis=1)
    p, k_store, v, q_bf, k_bf, v_bf, pool_bf = _inproj(
        x2d, pprev, w["norm_mix"], w["w_in"], w["q_gain"], w["k_gain"], w["group_ones"],
        w["w_pool"], w["pool_scale"], seq=t, pos0=past)
    k_all = jnp.concatenate([k_prev.reshape(b, past, QK_WIDTH).astype(BF16), k_bf.reshape(b, t, QK_WIDTH)], axis=1)
    v_all = jnp.concatenate([v_prev.reshape(b, past, ATTN_WIDTH).astype(BF16), v_bf.reshape(b, t, ATTN_WIDTH)], axis=1)
    att = _attention(q_bf.reshape(b, t, QK_WIDTH), k_all, v_all, lam_vecs, w["subln"], pos0=past)
    h2d, hn_t, scores_t = _outproj(x2d, pool_bf, att.reshape(n, ATTN_WIDTH), w["w_out"], w["norm_ffn"],
                                   w["w_peer_q"], w["sub_keys"])
    cnt, ea, rank, eb = _topk(scores_t)
    y = _peer(hn_t, h2d, w["peer_u"], w["peer_vt"], cnt, ea, rank, eb)
    tail = jnp.concatenate([pool_prev, p.reshape(b, t, POOL_WIDTH)], axis=1)[:, -POOL_TAIL:]
    return (y.reshape(b, t, D_MODEL), k_store.reshape(b, t, ATTN_HEADS, 2 * QK_DIM),
            v.reshape(b, t, ATTN_HEADS, V_DIM), tail)


def kernel(x_prompt, x_sample, cache_k, cache_v, state_pool, norm_mix, w_in, q_norm, k_norm, lambda_q1, lambda_k1, lambda_q2, lambda_k2, subln, w_pool, pool_scale, w_out, norm_ffn, w_peer_q, peer_sub_keys, peer_u, peer_v):
    depth = w_in.shape[0]
    assert depth == 1
    l = 0
    group = jnp.arange(QK_WIDTH) // QK_DIM
    w = {
        "norm_mix": norm_mix[l][None, :],
        "w_in": w_in[l].astype(BF16),
        "q_gain": jnp.tile(q_norm[l], QK_WIDTH // QK_DIM)[None, :],
        "k_gain": jnp.tile(k_norm[l], QK_WIDTH // QK_DIM)[None, :],
        "group_ones": (group[:, None] == group[None, :]).astype(BF16),
        "w_pool": w_pool[l].astype(BF16),
        "pool_scale": pool_scale[l][None, :],
        "subln": subln[l][None, :],
        "w_out": w_out[l].astype(BF16),
        "norm_ffn": norm_ffn[l][None, :],
        "w_peer_q": w_peer_q[l].astype(BF16),
        "sub_keys": peer_sub_keys[l].reshape(2 * PEER_HEADS, PEER_KEYS, PEER_HALF).astype(BF16),
        "peer_u": peer_u[l].astype(BF16),
        "peer_vt": peer_v[l].T.astype(BF16),
    }
    lam_vecs = jnp.stack([lambda_q1[l], lambda_k1[l], lambda_q2[l], lambda_k2[l]]).astype(F32)
    bp = x_prompt.shape[0]
    zero_pool = jnp.zeros((bp, POOL_TAIL, POOL_WIDTH), F32)
    zero_k = jnp.zeros((bp, 0, ATTN_HEADS, 2 * QK_DIM), F32)
    zero_v = jnp.zeros((bp, 0, ATTN_HEADS, V_DIM), F32)
    yp, kp, vp, pp = _stream(x_prompt, zero_pool, zero_k, zero_v, w, lam_vecs)
    ys, kn, vn, pn = _stream(x_sample, state_pool[l], cache_k[l], cache_v[l], w, lam_vecs)
    return (yp, ys, kp[None], vp[None], pp[None], kn[None], vn[None], pn[None])
```

```python
import functools
import math

import jax
import jax.numpy as jnp
from jax import lax
from jax.experimental import pallas as pl
from jax.experimental.pallas import tpu as pltpu

F32 = jnp.float32
BF16 = jnp.bfloat16

D_MODEL = 1024
CHUNK = 64
POOL_WIDTH = 512
POOL_WINDOWS = (2, 4, 8, 16)
POOL_GROUP = 128
POOL_TAIL = 15
POOL_HALO = 16
ATTN_HEADS = 4
QK_DIM = 64
V_DIM = 128
QK_WIDTH = 512
ATTN_WIDTH = 512
IN_WIDTH = 2048
ALIBI_SLOPES = tuple(2.0 ** (-8.0 * (h + 1) / ATTN_HEADS) for h in range(ATTN_HEADS))
PEER_HEADS = 8
PEER_KEYS = 128
PEER_EXPERTS = PEER_KEYS * PEER_KEYS
PEER_HALF = 128
PEER_TOPK = 16
NORM_EPS = 1e-6
NEG_INF = -1e30
LAM_INIT = 0.8 - 0.6 * math.exp(-0.3 * 0)
LOG2E = 1.4426950408889634
KEY_BLOCK = 512
POS_PERIOD = 256
MAX_SAFE_BOUND = 60.0

VMEM_LIMIT_BYTES = 56 * 1024 * 1024

_CAND_PIECES = ((0, 1, 16), (1, 1, 8), (2, 1, 5), (3, 1, 4), (4, 1, 3), (5, 1, 2), (6, 1, 2), (7, 1, 2), (8, 8, 1))
_CAND_ROWS = 80


def _params(semantics):
    return pltpu.CompilerParams(dimension_semantics=semantics, vmem_limit_bytes=VMEM_LIMIT_BYTES)


def _rms_rows(x, gain):
    ms = jnp.mean(x * x, axis=-1, keepdims=True)
    return x * lax.rsqrt(ms + NORM_EPS) * gain


def _dot_nt(a, b):
    return lax.dot_general(a, b, (((1,), (1,)), ((), ())), preferred_element_type=F32)


def _inproj_kernel(x_ref, g_ref, win_ref, qg_ref, kg_ref, gm_ref, pprev_ref, wpool_ref, pscale_ref,
                   p_ref, ks_ref, v_ref, qb_ref, kb_ref, vb_ref, pool_ref, ext_ref,
                   *, tm, tiles_per_batch, pos0):
    t_in_batch = lax.rem(pl.program_id(0), tiles_per_batch)
    hn = _rms_rows(x_ref[...], g_ref[...]).astype(BF16)
    proj = jnp.dot(hn, win_ref[...], preferred_element_type=F32)
    p = proj[:, :POOL_WIDTH]
    q = proj[:, POOL_WIDTH:POOL_WIDTH + QK_WIDTH]
    k = proj[:, POOL_WIDTH + QK_WIDTH:POOL_WIDTH + 2 * QK_WIDTH]
    v = proj[:, POOL_WIDTH + 2 * QK_WIDTH:]

    def group_rms(z, gain):
        sq = z * z
        hi = sq.astype(BF16)
        lo = (sq - hi.astype(F32)).astype(BF16)
        ss = (jnp.dot(hi, gm_ref[...], preferred_element_type=F32)
              + jnp.dot(lo, gm_ref[...], preferred_element_type=F32))
        return z * lax.rsqrt(ss * (1.0 / QK_DIM) + NORM_EPS) * gain

    qn = group_rms(q, qg_ref[...])
    kn = group_rms(k, kg_ref[...])
    p_ref[...] = p
    ks_ref[...] = kn
    v_ref[...] = v
    qb_ref[...] = (qn * (QK_DIM ** -0.5 * LOG2E)).astype(BF16)
    kb_ref[...] = kn.astype(BF16)
    vb_ref[...] = v.astype(BF16)

    @pl.when(t_in_batch == 0)
    def _():
        ext_ref[0:POOL_HALO, :] = pprev_ref[0]

    ext_ref[POOL_HALO:POOL_HALO + tm, :] = p
    pos = lax.broadcasted_iota(jnp.int32, (tm, POOL_GROUP), 0) + (pos0 + t_in_batch * tm)
    mixed = []
    for g, w in enumerate(POOL_WINDOWS):
        lanes = slice(g * POOL_GROUP, (g + 1) * POOL_GROUP)
        win = ext_ref[POOL_HALO:POOL_HALO + tm, lanes]
        for j in range(1, w):
            win = win + ext_ref[POOL_HALO - j:POOL_HALO - j + tm, lanes]
        cnt = jnp.minimum(w, pos + 1).astype(F32)
        pooled = win / cnt - p[:, lanes]
        mixed.append(jnp.dot(pooled.astype(BF16), wpool_ref[g], preferred_element_type=F32))
    pool_ref[...] = (jnp.concatenate(mixed, axis=1) * pscale_ref[...]).astype(BF16)
    ext_ref[0:POOL_HALO, :] = ext_ref[tm:tm + POOL_HALO, :]


def _inproj(x2d, pprev, norm_mix, w_in_bf, qg, kg, gm, w_pool_bf, pool_scale, *, seq, pos0):
    n = x2d.shape[0]
    tm = min(seq, 512)
    tiles_per_batch = seq // tm
    row = lambda i: (i, 0)
    full2 = lambda i: (0, 0)
    out_w = lambda dt: jax.ShapeDtypeStruct((n, POOL_WIDTH), dt)
    return pl.pallas_call(
        functools.partial(_inproj_kernel, tm=tm, tiles_per_batch=tiles_per_batch, pos0=pos0),
        grid=(n // tm,),
        in_specs=[
            pl.BlockSpec((tm, D_MODEL), row),
            pl.BlockSpec((1, D_MODEL), full2),
            pl.BlockSpec((D_MODEL, IN_WIDTH), full2),
            pl.BlockSpec((1, QK_WIDTH), full2),
            pl.BlockSpec((1, QK_WIDTH), full2),
            pl.BlockSpec((QK_WIDTH, QK_WIDTH), full2),
            pl.BlockSpec((1, POOL_HALO, POOL_WIDTH), lambda i: (i // tiles_per_batch, 0, 0)),
            pl.BlockSpec((len(POOL_WINDOWS), POOL_GROUP, POOL_GROUP), lambda i: (0, 0, 0)),
            pl.BlockSpec((1, POOL_WIDTH), full2),
        ],
        out_specs=[pl.BlockSpec((tm, POOL_WIDTH), row)] * 7,
        out_shape=[out_w(F32), out_w(F32), out_w(F32), out_w(BF16), out_w(BF16), out_w(BF16), out_w(BF16)],
        scratch_shapes=[pltpu.VMEM((tm + POOL_HALO, POOL_WIDTH), F32)],
        compiler_params=_params(("arbitrary",)),
        name="inproj_pool",
    )(x2d, norm_mix, w_in_bf, qg, kg, gm, pprev, w_pool_bf, pool_scale)


_AUG_START = (QK_DIM, 0)


def _keyprep_kernel(k_ref, gm_ref, kp_ref, kmax_ref, *, tm):
    t = pl.program_id(1)
    k = k_ref[0].astype(F32)
    ss = jnp.dot((k * k).astype(BF16), gm_ref[...], preferred_element_type=F32)
    tile_max = jnp.max(ss, axis=0, keepdims=True)

    @pl.when(t == 0)
    def _():
        kmax_ref[0] = tile_max

    @pl.when(t > 0)
    def _():
        kmax_ref[0] = jnp.maximum(kmax_ref[0], tile_max)

    lane = lax.broadcasted_iota(jnp.int32, (tm, V_DIM), 1)
    pos = ((lax.broadcasted_iota(jnp.int32, (tm, V_DIM), 0) + t * tm) & (POS_PERIOD - 1)).astype(F32)
    for h in range(ATTN_HEADS):
        kh = k[:, h * V_DIM:(h + 1) * V_DIM]
        for sub in range(2):
            a0 = _AUG_START[sub]
            aug = jnp.where((lane >= a0) & (lane < a0 + 3), 1.0,
                            jnp.where((lane >= a0 + 3) & (lane < a0 + 6), pos, 0.0))
            own = (lane < QK_DIM) if sub == 0 else (lane >= QK_DIM)
            g = 2 * h + sub
            kp_ref[0, :, g * V_DIM:(g + 1) * V_DIM] = jnp.where(own, kh, aug).astype(BF16)


def _keyprep(k3d, gm):
    b, s, _ = k3d.shape
    tm = min(s, 512)
    return pl.pallas_call(
        functools.partial(_keyprep_kernel, tm=tm),
        grid=(b, s // tm),
        in_specs=[
            pl.BlockSpec((1, tm, QK_WIDTH), lambda bi, ti: (bi, ti, 0)),
            pl.BlockSpec((QK_WIDTH, QK_WIDTH), lambda bi, ti: (0, 0)),
        ],
        out_specs=[
            pl.BlockSpec((1, tm, 2 * QK_WIDTH), lambda bi, ti: (bi, ti, 0)),
            pl.BlockSpec((1, 1, QK_WIDTH), lambda bi, ti: (bi, 0, 0)),
        ],
        out_shape=[
            jax.ShapeDtypeStruct((b, s, 2 * QK_WIDTH), BF16),
            jax.ShapeDtypeStruct((b, 1, QK_WIDTH), F32),
        ],
        compiler_params=_params(("arbitrary", "arbitrary")),
        name="attn_keyprep",
    )(k3d, gm)


def _split3(a):
    hi = a.astype(BF16).astype(F32)
    r = a - hi
    mid = r.astype(BF16).astype(F32)
    return hi, mid, r - mid


def _diff_finish(acc, l, lam_ref, subln_ref, tq):
    lam_v = lam_ref[...]
    lam = (jnp.exp(jnp.sum(lam_v[0:1] * lam_v[1:2], axis=1, keepdims=True))
           - jnp.exp(jnp.sum(lam_v[2:3] * lam_v[3:4], axis=1, keepdims=True)) + LAM_INIT)
    o = acc / l
    o = o[:tq] - lam * o[tq:]
    return (_rms_rows(o, subln_ref[...]) * (1.0 - LAM_INIT)).astype(BF16)


def _attn_fast_kernel(tab_ref, q_ref, kpast_ref, vpast_ref, kdiag_ref, vdiag_ref, kmax_ref, lam_ref, subln_ref,
                      o_ref, bound_ref, acc_ref, l_ref, bdiag_ref, s_ref, *, tq, tk, pos0):
    h = pl.program_id(1)
    qi = pl.program_id(2)
    slope = tab_ref[h, 0]

    @pl.when(qi == 0)
    def _():
        r = lax.rem(lax.broadcasted_iota(jnp.int32, (2 * tq, tq), 0), tq)
        c = lax.broadcasted_iota(jnp.int32, (2 * tq, tq), 1)
        allowed = (c // CHUNK) <= (r // CHUNK)
        bdiag_ref[...] = jnp.where(allowed, -slope * jnp.abs(r - c).astype(F32), NEG_INF)

    q = q_ref[0].astype(F32)
    lane = lax.broadcasted_iota(jnp.int32, q.shape, 1)
    first = lane < QK_DIM
    qsq = q * q
    km = kmax_ref[0]
    bounds = (jnp.sqrt(jnp.sum(jnp.where(first, qsq, 0.0), axis=1, keepdims=True) * km[:, 0:1]),
              jnp.sqrt(jnp.sum(jnp.where(first, 0.0, qsq), axis=1, keepdims=True) * km[:, QK_DIM:QK_DIM + 1]))
    bound_ref[0, 0, 0] = jnp.broadcast_to(
        jnp.maximum(jnp.max(bounds[0], axis=0, keepdims=True), jnp.max(bounds[1], axis=0, keepdims=True)),
        (8, V_DIM))
    row = lax.broadcasted_iota(jnp.int32, (tq, 1), 0).astype(F32)

    def augmented(sub, with_alibi):
        a0 = _AUG_START[sub]
        offset = -(slope * row + bounds[sub]) if with_alibi else -bounds[sub]
        pieces = _split3(offset) + ((tab_ref[h, 1], tab_ref[h, 2], tab_ref[h, 3]) if with_alibi else ())
        aug = jnp.zeros(q.shape, F32)
        for i, piece in enumerate(pieces):
            aug = jnp.where(lane == a0 + i, piece, aug)
        own = first if sub == 0 else jnp.logical_not(first)
        return jnp.where(own, q, aug).astype(BF16)

    acc_ref[...] = jnp.zeros_like(acc_ref)
    l_ref[...] = jnp.zeros_like(l_ref)

    def consume(s, vblk):
        p = jnp.exp2(s)
        width = p.shape[1]
        if width >= V_DIM:
            part = p[:, 0:V_DIM]
            for c0 in range(V_DIM, width, V_DIM):
                part = part + p[:, c0:c0 + V_DIM]
            l_ref[...] += part
        else:
            l_ref[:, 0:width] += p
        acc_ref[...] += jnp.dot(p.astype(BF16), vblk, preferred_element_type=F32)

    base = pos0 + qi * tq
    q_past = (augmented(0, True), augmented(1, True))

    n_past = base // tk
    col = lax.broadcasted_iota(jnp.int32, (1, tk), 1)
    col_offset = slope * (col - (col & (POS_PERIOD - 1))).astype(F32)

    def past_exponents(kb):
        start = pl.multiple_of(kb * tk, tk)
        s = jnp.concatenate([_dot_nt(q_past[0], kpast_ref[0, pl.ds(start, tk), 0:V_DIM]),
                             _dot_nt(q_past[1], kpast_ref[0, pl.ds(start, tk), V_DIM:2 * V_DIM])], axis=0)
        return s + (col_offset - slope * (base - kb * tk).astype(F32))

    def past_values(kb):
        return vpast_ref[0, pl.ds(pl.multiple_of(kb * tk, tk), tk), :]

    @pl.when(n_past > 0)
    def _():
        s_ref[...] = past_exponents(0)

    def past_block(kb, _):
        s_next = past_exponents(kb + 1)
        consume(s_ref[...], past_values(kb))
        s_ref[...] = s_next
        return 0

    lax.fori_loop(0, n_past - 1, past_block, 0)

    @pl.when(n_past > 0)
    def _():
        consume(s_ref[...], past_values(n_past - 1))

    s = jnp.concatenate([_dot_nt(augmented(0, False), kdiag_ref[0, :, 0:V_DIM]),
                         _dot_nt(augmented(1, False), kdiag_ref[0, :, V_DIM:2 * V_DIM])], axis=0)
    consume(s + bdiag_ref[...], vdiag_ref[0])
    l = jnp.sum(l_ref[...], axis=1, keepdims=True)
    o_ref[0] = _diff_finish(acc_ref[...], l, lam_ref, subln_ref, tq)


def _attention_fast(tab, q_bf, kp_past, v_past, kp_new, v_new, kmax, lam_vecs, subln, *, pos0):
    b, t, _ = q_bf.shape
    s_past = kp_past.shape[1]
    tq = min(t, KEY_BLOCK)
    tk = KEY_BLOCK
    nq = t // tq
    assert pos0 % tk == 0 and t % tq == 0 and (tq == tk or nq == 1) and pos0 % CHUNK == 0 and tq % 8 == 0
    assert s_past >= pos0 + t - tq
    at_q = lambda bi, hi, qi: (bi, qi, hi)
    at_head = lambda bi, hi, qi: (bi, 0, hi)
    const2 = lambda bi, hi, qi: (0, 0)
    return pl.pallas_call(
        functools.partial(_attn_fast_kernel, tq=tq, tk=tk, pos0=pos0),
        grid=(b, ATTN_HEADS, nq),
        in_specs=[
            pl.BlockSpec(memory_space=pltpu.SMEM),
            pl.BlockSpec((1, tq, V_DIM), at_q),
            pl.BlockSpec((1, s_past, 2 * V_DIM), at_head),
            pl.BlockSpec((1, s_past, V_DIM), at_head),
            pl.BlockSpec((1, tq, 2 * V_DIM), at_q),
            pl.BlockSpec((1, tq, V_DIM), at_q),
            pl.BlockSpec((1, 1, V_DIM), at_head),
            pl.BlockSpec((4, QK_DIM), const2),
            pl.BlockSpec((1, V_DIM), const2),
        ],
        out_specs=[
            pl.BlockSpec((1, tq, V_DIM), at_q),
            pl.BlockSpec((1, 1, 1, 8, V_DIM), lambda bi, hi, qi: (bi, hi, qi, 0, 0)),
        ],
        out_shape=[
            jax.ShapeDtypeStruct((b, t, ATTN_WIDTH), BF16),
            jax.ShapeDtypeStruct((b, ATTN_HEADS, nq, 8, V_DIM), F32),
        ],
        scratch_shapes=[pltpu.VMEM((2 * tq, V_DIM), F32), pltpu.VMEM((2 * tq, V_DIM), F32),
                        pltpu.VMEM((2 * tq, tq), F32), pltpu.VMEM((2 * tq, tk), F32)],
        compiler_params=_params(("arbitrary", "arbitrary", "arbitrary")),
        name="diff_attention",
    )(tab, q_bf, kp_past, v_past, kp_new, v_new, kmax, lam_vecs, subln)


def _attn_online_kernel(tab_ref, q_ref, k_ref, v_ref, lam_ref, subln_ref, o_ref, boff_ref, bdiag_ref,
                        *, tq, tk, pos0):
    h = pl.program_id(1)
    qi = pl.program_id(2)
    slope = tab_ref[h, 0]

    @pl.when(qi == 0)
    def _():
        r = lax.rem(lax.broadcasted_iota(jnp.int32, (2 * tq, tk), 0), tq)
        c = lax.broadcasted_iota(jnp.int32, (2 * tq, tk), 1)
        boff_ref[...] = -slope * (r - c).astype(F32)
        r = lax.rem(lax.broadcasted_iota(jnp.int32, (2 * tq, tq), 0), tq)
        c = lax.broadcasted_iota(jnp.int32, (2 * tq, tq), 1)
        allowed = (c // CHUNK) <= (r // CHUNK)
        bdiag_ref[...] = jnp.where(allowed, -slope * jnp.abs(r - c).astype(F32), NEG_INF)

    q = q_ref[0]
    lane = lax.broadcasted_iota(jnp.int32, q.shape, 1)
    zero = jnp.zeros_like(q)
    qs = jnp.concatenate([jnp.where(lane < QK_DIM, q, zero), jnp.where(lane >= QK_DIM, q, zero)], axis=0)
    base = pos0 + qi * tq

    def update(s, vblk, carry):
        m, l, acc = carry
        m_new = jnp.maximum(m, jnp.max(s, axis=1, keepdims=True))
        alpha = jnp.exp2(m - m_new)
        p = jnp.exp2(s - m_new)
        l = alpha * l + jnp.sum(p, axis=1, keepdims=True)
        acc = alpha * acc + jnp.dot(p.astype(BF16), vblk, preferred_element_type=F32)
        return m_new, l, acc

    def past_block(kb, carry):
        start = pl.multiple_of(kb * tk, tk)
        s = _dot_nt(qs, k_ref[0, pl.ds(start, tk), :])
        s = s + boff_ref[...] - slope * (base - kb * tk).astype(F32)
        return update(s, v_ref[0, pl.ds(start, tk), :], carry)

    carry = (jnp.full((2 * tq, 1), NEG_INF, F32), jnp.zeros((2 * tq, 1), F32), jnp.zeros((2 * tq, V_DIM), F32))
    carry = lax.fori_loop(0, base // tk, past_block, carry)
    dstart = pl.multiple_of(base, tq)
    s = _dot_nt(qs, k_ref[0, pl.ds(dstart, tq), :]) + bdiag_ref[...]
    _, l, acc = update(s, v_ref[0, pl.ds(dstart, tq), :], carry)
    o_ref[0] = _diff_finish(acc, l, lam_ref, subln_ref, tq)


def _attention_online(tab, q_bf, k_all, v_all, lam_vecs, subln, *, pos0):
    b, t, _ = q_bf.shape
    s = k_all.shape[1]
    tq = min(t, KEY_BLOCK)
    tk = KEY_BLOCK
    assert pos0 % tk == 0 and t % tq == 0 and (tq == tk or t == tq) and pos0 % CHUNK == 0 and tq % 8 == 0
    assert s == pos0 + t
    return pl.pallas_call(
        functools.partial(_attn_online_kernel, tq=tq, tk=tk, pos0=pos0),
        grid=(b, ATTN_HEADS, t // tq),
        in_specs=[
            pl.BlockSpec(memory_space=pltpu.SMEM),
            pl.BlockSpec((1, tq, V_DIM), lambda bi, hi, qi: (bi, qi, hi)),
            pl.BlockSpec((1, s, V_DIM), lambda bi, hi, qi: (bi, 0, hi)),
            pl.BlockSpec((1, s, V_DIM), lambda bi, hi, qi: (bi, 0, hi)),
            pl.BlockSpec((4, QK_DIM), lambda bi, hi, qi: (0, 0)),
            pl.BlockSpec((1, V_DIM), lambda bi, hi, qi: (0, 0)),
        ],
        out_specs=pl.BlockSpec((1, tq, V_DIM), lambda bi, hi, qi: (bi, qi, hi)),
        out_shape=jax.ShapeDtypeStruct((b, t, ATTN_WIDTH), BF16),
        scratch_shapes=[pltpu.VMEM((2 * tq, tk), F32), pltpu.VMEM((2 * tq, tq), F32)],
        compiler_params=_params(("arbitrary", "arbitrary", "arbitrary")),
        name="diff_attention_online",
    )(tab, q_bf, k_all, v_all, lam_vecs, subln)


def _alibi_table():
    cs = jnp.asarray([LOG2E * s for s in ALIBI_SLOPES], F32)
    return jnp.stack((cs,) + _split3(cs), axis=1)


def _attention(q_bf, k_prev_bf, v_prev_bf, k_bf, v_bf, gm, lam_vecs, subln, *, pos0):
    tab = _alibi_table()
    kp_new, kmax = _keyprep(k_bf, gm)
    if pos0:
        kp_past, kmax_past = _keyprep(k_prev_bf, gm)
        kmax = jnp.maximum(kmax, kmax_past)
        v_past = v_prev_bf
    else:
        kp_past, v_past = kp_new, v_bf
    att, bound = _attention_fast(tab, q_bf, kp_past, v_past, kp_new, v_bf, kmax, lam_vecs, subln, pos0=pos0)

    def online():
        k_all = jnp.concatenate([k_prev_bf, k_bf], axis=1) if pos0 else k_bf
        v_all = jnp.concatenate([v_prev_bf, v_bf], axis=1) if pos0 else v_bf
        return _attention_online(tab, q_bf, k_all, v_all, lam_vecs, subln, pos0=pos0)

    return lax.cond(jnp.max(bound) <= MAX_SAFE_BOUND, lambda: att, online)


def _outproj_kernel(x_ref, pool_ref, att_ref, wout_ref, g_ref, wq_ref, sk_ref, h_ref, hnt_ref, st_ref):
    mix = (jnp.dot(pool_ref[...], wout_ref[0:POOL_WIDTH, :], preferred_element_type=F32)
           + jnp.dot(att_ref[...], wout_ref[POOL_WIDTH:, :], preferred_element_type=F32))
    h = x_ref[...] + mix
    h_ref[...] = h
    hn = _rms_rows(h, g_ref[...])
    hnt_ref[...] = hn.T.astype(BF16)
    q = jnp.dot(hn.astype(BF16), wq_ref[...], preferred_element_type=F32)
    for g in range(2 * PEER_HEADS):
        qg = q[:, g * PEER_HALF:(g + 1) * PEER_HALF].astype(BF16)
        st_ref[g] = _dot_nt(sk_ref[g], qg)


def _outproj(x2d, pool_bf, att_bf, w_out_bf, norm_ffn, w_q_bf, sub_keys_bf):
    n = x2d.shape[0]
    tm = min(n, 256)
    row = lambda i: (i, 0)
    full2 = lambda i: (0, 0)
    return pl.pallas_call(
        _outproj_kernel,
        grid=(n // tm,),
        in_specs=[
            pl.BlockSpec((tm, D_MODEL), row),
            pl.BlockSpec((tm, POOL_WIDTH), row),
            pl.BlockSpec((tm, ATTN_WIDTH), row),
            pl.BlockSpec((D_MODEL, D_MODEL), full2),
            pl.BlockSpec((1, D_MODEL), full2),
            pl.BlockSpec((D_MODEL, 2 * PEER_HEADS * PEER_HALF), full2),
            pl.BlockSpec((2 * PEER_HEADS, PEER_KEYS, PEER_HALF), lambda i: (0, 0, 0)),
        ],
        out_specs=[
            pl.BlockSpec((tm, D_MODEL), row),
            pl.BlockSpec((D_MODEL, tm), lambda i: (0, i)),
            pl.BlockSpec((2 * PEER_HEADS, PEER_KEYS, tm), lambda i: (0, 0, i)),
        ],
        out_shape=[
            jax.ShapeDtypeStruct((n, D_MODEL), F32),
            jax.ShapeDtypeStruct((D_MODEL, n), BF16),
            jax.ShapeDtypeStruct((2 * PEER_HEADS, PEER_KEYS, n), F32),
        ],
        compiler_params=_params(("arbitrary",)),
        name="outproj_peer_query",
    )(x2d, pool_bf, att_bf, w_out_bf, norm_ffn, w_q_bf, sub_keys_bf)


def _top16_rows(s, iota_k):
    tn = s.shape[1]
    row16 = lax.broadcasted_iota(jnp.int32, (PEER_TOPK, tn), 0)
    rank = jnp.full(s.shape, float(PEER_TOPK), F32)
    vals = jnp.zeros((PEER_TOPK, tn), F32)
    cur = s
    for i in range(PEER_TOPK):
        m = jnp.max(cur, axis=0, keepdims=True)
        idx = jnp.min(jnp.where(cur == m, iota_k, s.shape[0]), axis=0, keepdims=True)
        hit = iota_k == idx
        rank = jnp.where(hit, float(i), rank)
        cur = jnp.where(hit, NEG_INF, cur)
        vals = jnp.where(row16 == i, m, vals)
    return vals, rank


def _topk_kernel(s_ref, cnt_ref, ea_ref, rank_ref, eb_ref, *, tn):
    iota_k = lax.broadcasted_iota(jnp.int32, (PEER_KEYS, tn), 0)
    iota_c = lax.broadcasted_iota(jnp.int32, (_CAND_ROWS, tn), 0)
    row8 = lax.broadcasted_iota(jnp.int32, (8, tn), 0)
    for h in range(PEER_HEADS):
        sa = s_ref[2 * h]
        sb = s_ref[2 * h + 1]
        va, rank_a = _top16_rows(sa, iota_k)
        vb, rank_b = _top16_rows(sb, iota_k)
        pieces = []
        for i0, ni, nj in _CAND_PIECES:
            if ni == 1:
                rows = 16 if nj > 8 else 8
                piece = va[i0:i0 + 1] + vb[0:rows]
                if nj < rows:
                    piece = jnp.where(row8 < nj, piece, NEG_INF)
            else:
                piece = va[i0:i0 + ni] + vb[0:1]
            pieces.append(piece)
        cand = jnp.concatenate(pieces, axis=0)
        cur = cand
        sel = jnp.zeros(cand.shape, F32)
        for _ in range(PEER_TOPK):
            m = jnp.max(cur, axis=0, keepdims=True)
            idx = jnp.min(jnp.where(cur == m, iota_c, _CAND_ROWS), axis=0, keepdims=True)
            hit = iota_c == idx
            sel = jnp.where(hit, 1.0, sel)
            cur = jnp.where(hit, NEG_INF, cur)
        z = jnp.sum(jnp.where(sel > 0.0, jnp.exp(cand - cand[0:1]), 0.0), axis=0, keepdims=True)
        cnt = jnp.zeros(sa.shape, F32)
        off = 0
        for i0, ni, nj in _CAND_PIECES:
            rows = (16 if nj > 8 else 8) if ni == 1 else ni
            if ni == 1:
                c_i = jnp.sum(sel[off:off + rows], axis=0, keepdims=True)
                cnt = jnp.where(rank_a == float(i0), c_i, cnt)
            else:
                for r in range(ni):
                    cnt = jnp.where(rank_a == float(i0 + r), sel[off + r:off + r + 1], cnt)
            off += rows
        cnt_ref[h] = cnt
        ea_ref[h] = jnp.exp(sa - va[0:1])
        rank_ref[h] = rank_b.astype(BF16)
        eb_ref[h] = (jnp.exp(sb - vb[0:1]) / z).astype(BF16)


def _topk(scores_t):
    n = scores_t.shape[2]
    tn = min(n, 128)
    spec_in = pl.BlockSpec((2 * PEER_HEADS, PEER_KEYS, tn), lambda i: (0, 0, i))
    spec_out = pl.BlockSpec((PEER_HEADS, PEER_KEYS, tn), lambda i: (0, 0, i))
    shape_out = lambda dt: jax.ShapeDtypeStruct((PEER_HEADS, PEER_KEYS, n), dt)
    return pl.pallas_call(
        functools.partial(_topk_kernel, tn=tn),
        grid=(n // tn,),
        in_specs=[spec_in],
        out_specs=[spec_out] * 4,
        out_shape=[shape_out(F32), shape_out(F32), shape_out(BF16), shape_out(BF16)],
        compiler_params=_params(("arbitrary",)),
        name="peer_topk",
    )(scores_t)


def _peer_kernel(hnt_ref, h_ref, u_ref, vta_ref, vtb_ref, cnt_ref, ea_ref, rank_ref, eb_ref, y_ref,
                 acc_ref, acta_ref, actb_ref, *, te):
    step = pl.program_id(1)
    last = pl.num_programs(1) - 1

    @pl.when(step == 0)
    def _():
        acc_ref[...] = jnp.zeros_like(acc_ref)
        actb_ref[...] = jnp.zeros_like(actb_ref)

    hnt = hnt_ref[...]
    tn = hnt.shape[1]
    pack = 16
    tiles = PEER_KEYS // pack
    zero = jnp.zeros((tiles, pack, tn), BF16)

    def key_row(ref, h, a_glob):
        return jnp.broadcast_to(ref[h, pl.ds(a_glob, 1), :], (pack, tn)).astype(BF16)[None]

    def contribution(act_ref, vt_ref, block):
        coefs = []
        for r0 in range(0, te, PEER_KEYS):
            a_glob = block * (te // PEER_KEYS) + r0 // PEER_KEYS
            gate = None
            for h in range(PEER_HEADS):
                cnt = key_row(cnt_ref, h, a_glob)
                rank = rank_ref[h].reshape(tiles, pack, tn)
                term = jnp.where(rank < cnt, eb_ref[h].reshape(tiles, pack, tn), zero) * key_row(ea_ref, h, a_glob)
                gate = term if gate is None else gate + term
            x = act_ref[r0:r0 + PEER_KEYS, :]
            gelu = 0.5 * x * (1.0 + lax.erf(x * (2.0 ** -0.5)))
            coefs.append(gate.reshape(PEER_KEYS, tn) * gelu.astype(BF16))
        return jnp.dot(vt_ref[...], jnp.concatenate(coefs, axis=0), preferred_element_type=F32)

    acta_ref[...] = jnp.dot(u_ref[0:te, :], hnt, preferred_element_type=F32)
    part_b = contribution(actb_ref, vtb_ref, jnp.maximum(2 * step - 1, 0))
    actb_ref[...] = jnp.dot(u_ref[te:2 * te, :], hnt, preferred_element_type=F32)
    part_a = contribution(acta_ref, vta_ref, jnp.minimum(2 * step, 2 * last - 1))
    acc_ref[...] += jnp.where(step > 0, part_b, 0.0) + jnp.where(step < last, part_a, 0.0)

    @pl.when(step == last)
    def _():
        y_ref[...] = h_ref[...] + acc_ref[...].T


def _peer(hn_t, h2d, u_bf, vt_bf, cnt, ea, rank, eb):
    n = h2d.shape[0]
    tn = min(n, 512)
    te = 512
    n_pairs = PEER_EXPERTS // (2 * te)
    tok = lambda i, s: (0, 0, i)
    fac = pl.BlockSpec((PEER_HEADS, PEER_KEYS, tn), tok)
    return pl.pallas_call(
        functools.partial(_peer_kernel, te=te),
        grid=(n // tn, n_pairs + 1),
        in_specs=[
            pl.BlockSpec((D_MODEL, tn), lambda i, s: (0, i)),
            pl.BlockSpec((tn, D_MODEL), lambda i, s: (i, 0)),
            pl.BlockSpec((2 * te, D_MODEL), lambda i, s: (jnp.minimum(s, n_pairs - 1), 0)),
            pl.BlockSpec((D_MODEL, te), lambda i, s: (0, jnp.minimum(2 * s, 2 * n_pairs - 1))),
            pl.BlockSpec((D_MODEL, te), lambda i, s: (0, jnp.maximum(2 * s - 1, 0))),
            fac, fac, fac, fac,
        ],
        out_specs=pl.BlockSpec((tn, D_MODEL), lambda i, s: (i, 0)),
        out_shape=jax.ShapeDtypeStruct((n, D_MODEL), F32),
        scratch_shapes=[pltpu.VMEM((D_MODEL, tn), F32), pltpu.VMEM((te, tn), F32), pltpu.VMEM((te, tn), F32)],
        compiler_params=_params(("arbitrary", "arbitrary")),
        name="peer_dense",
    )(hn_t, h2d, u_bf, vt_bf, vt_bf, cnt, ea, rank, eb)


def _stream(x, pool_prev, k_prev, v_prev, w, lam_vecs):
    b, t, _ = x.shape
    past = k_prev.shape[1]
    n = b * t
    x2d = x.reshape(n, D_MODEL)
    pprev = jnp.concatenate([jnp.zeros((b, POOL_HALO - POOL_TAIL, POOL_WIDTH), F32), pool_prev], axis=1)
    p, k_store, v, q_bf, k_bf, v_bf, pool_bf = _inproj(
        x2d, pprev, w["norm_mix"], w["w_in"], w["q_gain"], w["k_gain"], w["group_ones"],
        w["w_pool"], w["pool_scale"], seq=t, pos0=past)
    att = _attention(q_bf.reshape(b, t, QK_WIDTH),
                     k_prev.reshape(b, past, QK_WIDTH).astype(BF16), v_prev.reshape(b, past, ATTN_WIDTH).astype(BF16),
                     k_bf.reshape(b, t, QK_WIDTH), v_bf.reshape(b, t, ATTN_WIDTH),
                     w["group_ones"], lam_vecs, w["subln"], pos0=past)
    h2d, hn_t, scores_t = _outproj(x2d, pool_bf, att.reshape(n, ATTN_WIDTH), w["w_out"], w["norm_ffn"],
                                   w["w_peer_q"], w["sub_keys"])
    cnt, ea, rank, eb = _topk(scores_t)
    y = _peer(hn_t, h2d, w["peer_u"], w["peer_vt"], cnt, ea, rank, eb)
    tail = jnp.concatenate([pool_prev, p.reshape(b, t, POOL_WIDTH)], axis=1)[:, -POOL_TAIL:]
    return (y.reshape(b, t, D_MODEL), k_store.reshape(b, t, ATTN_HEADS, 2 * QK_DIM),
            v.reshape(b, t, ATTN_HEADS, V_DIM), tail)


def kernel(x_prompt, x_sample, cache_k, cache_v, state_pool, norm_mix, w_in, q_norm, k_norm, lambda_q1, lambda_k1, lambda_q2, lambda_k2, subln, w_pool, pool_scale, w_out, norm_ffn, w_peer_q, peer_sub_keys, peer_u, peer_v):
    depth = w_in.shape[0]
    assert depth == 1
    l = 0
    group = jnp.arange(QK_WIDTH) // QK_DIM
    w = {
        "norm_mix": norm_mix[l][None, :],
        "w_in": w_in[l].astype(BF16),
        "q_gain": jnp.tile(q_norm[l], QK_WIDTH // QK_DIM)[None, :],
        "k_gain": jnp.tile(k_norm[l], QK_WIDTH // QK_DIM)[None, :],
        "group_ones": (group[:, None] == group[None, :]).astype(BF16),
        "w_pool": w_pool[l].astype(BF16),
        "pool_scale": pool_scale[l][None, :],
        "subln": subln[l][None, :],
        "w_out": w_out[l].astype(BF16),
        "norm_ffn": norm_ffn[l][None, :],
        "w_peer_q": w_peer_q[l].astype(BF16),
        "sub_keys": peer_sub_keys[l].reshape(2 * PEER_HEADS, PEER_KEYS, PEER_HALF).astype(BF16),
        "peer_u": peer_u[l].astype(BF16),
        "peer_vt": peer_v[l].T.astype(BF16),
    }
    lam_vecs = jnp.stack([lambda_q1[l], lambda_k1[l], lambda_q2[l], lambda_k2[l]]).astype(F32)
    bp = x_prompt.shape[0]
    zero_pool = jnp.zeros((bp, POOL_TAIL, POOL_WIDTH), F32)
    zero_k = jnp.zeros((bp, 0, ATTN_HEADS, 2 * QK_DIM), F32)
    zero_v = jnp.zeros((bp, 0, ATTN_HEADS, V_DIM), F32)
    yp, kp, vp, pp = _stream(x_prompt, zero_pool, zero_k, zero_v, w, lam_vecs)
    ys, kn, vn, pn = _stream(x_sample, state_pool[l], cache_k[l], cache_v[l], w, lam_vecs)
    return (yp, ys, kp[None], vp[None], pp[None], kn[None], vn[None], pn[None])
```

```python
import functools
import math

import jax
import jax.numpy as jnp
from jax import lax
from jax.experimental import pallas as pl
from jax.experimental.pallas import tpu as pltpu

F32 = jnp.float32
BF16 = jnp.bfloat16

D_MODEL = 1024
CHUNK = 64
POOL_WIDTH = 512
POOL_WINDOWS = (2, 4, 8, 16)
POOL_GROUP = 128
POOL_TAIL = 15
POOL_HALO = 16
ATTN_HEADS = 4
QK_DIM = 64
V_DIM = 128
QK_WIDTH = 512
ATTN_WIDTH = 512
IN_WIDTH = 2048
ALIBI_SLOPES = tuple(2.0 ** (-8.0 * (h + 1) / ATTN_HEADS) for h in range(ATTN_HEADS))
PEER_HEADS = 8
PEER_KEYS = 128
PEER_EXPERTS = PEER_KEYS * PEER_KEYS
PEER_HALF = 128
PEER_TOPK = 16
PEER_BLOCK = 512
NORM_EPS = 1e-6
NEG_INF = -1e30
LAM_INIT = 0.8 - 0.6 * math.exp(-0.3 * 0)
LOG2E = 1.4426950408889634
KEY_BLOCK = 512
POS_PERIOD = 256
MAX_SAFE_BOUND = 60.0

VMEM_LIMIT_BYTES = 56 * 1024 * 1024

_CAND_PIECES = ((0, 1, 16), (1, 1, 8), (2, 1, 5), (3, 1, 4), (4, 1, 3), (5, 1, 2), (6, 1, 2), (7, 1, 2), (8, 8, 1))
_CAND_ROWS = 80


def _params(semantics):
    return pltpu.CompilerParams(dimension_semantics=semantics, vmem_limit_bytes=VMEM_LIMIT_BYTES)


def _rms_rows(x, gain):
    ms = jnp.mean(x * x, axis=-1, keepdims=True)
    return x * lax.rsqrt(ms + NORM_EPS) * gain


def _dot_nt(a, b):
    return lax.dot_general(a, b, (((1,), (1,)), ((), ())), preferred_element_type=F32)


def _inproj_kernel(x_ref, g_ref, win_ref, qg_ref, kg_ref, gm_ref, pprev_ref, wpool_ref, pscale_ref,
                   p_ref, ks_ref, v_ref, qb_ref, kb_ref, vb_ref, pool_ref, ext_ref,
                   *, tm, tiles_per_batch, pos0):
    t_in_batch = lax.rem(pl.program_id(0), tiles_per_batch)
    hn = _rms_rows(x_ref[...], g_ref[...]).astype(BF16)
    proj = jnp.dot(hn, win_ref[...], preferred_element_type=F32)
    p = proj[:, :POOL_WIDTH]
    q = proj[:, POOL_WIDTH:POOL_WIDTH + QK_WIDTH]
    k = proj[:, POOL_WIDTH + QK_WIDTH:POOL_WIDTH + 2 * QK_WIDTH]
    v = proj[:, POOL_WIDTH + 2 * QK_WIDTH:]

    def group_rms(z, gain):
        sq = z * z
        hi = sq.astype(BF16)
        lo = (sq - hi.astype(F32)).astype(BF16)
        ss = (jnp.dot(hi, gm_ref[...], preferred_element_type=F32)
              + jnp.dot(lo, gm_ref[...], preferred_element_type=F32))
        return z * lax.rsqrt(ss * (1.0 / QK_DIM) + NORM_EPS) * gain

    qn = group_rms(q, qg_ref[...])
    kn = group_rms(k, kg_ref[...])
    p_ref[...] = p
    ks_ref[...] = kn
    v_ref[...] = v
    qb = (qn * (QK_DIM ** -0.5 * LOG2E)).astype(BF16)
    vb = v.astype(BF16)
    for h in range(ATTN_HEADS):
        qb_ref[h] = qb[:, h * V_DIM:(h + 1) * V_DIM]
        vb_ref[h] = vb[:, h * V_DIM:(h + 1) * V_DIM]
    kb_ref[...] = kn.astype(BF16)

    @pl.when(t_in_batch == 0)
    def _():
        ext_ref[0:POOL_HALO, :] = pprev_ref[0]

    ext_ref[POOL_HALO:POOL_HALO + tm, :] = p
    pos = lax.broadcasted_iota(jnp.int32, (tm, POOL_GROUP), 0) + (pos0 + t_in_batch * tm)
    mixed = []
    for g, w in enumerate(POOL_WINDOWS):
        lanes = slice(g * POOL_GROUP, (g + 1) * POOL_GROUP)
        win = ext_ref[POOL_HALO:POOL_HALO + tm, lanes]
        for j in range(1, w):
            win = win + ext_ref[POOL_HALO - j:POOL_HALO - j + tm, lanes]
        cnt = jnp.minimum(w, pos + 1).astype(F32)
        pooled = win / cnt - p[:, lanes]
        mixed.append(jnp.dot(pooled.astype(BF16), wpool_ref[g], preferred_element_type=F32))
    pool_ref[...] = (jnp.concatenate(mixed, axis=1) * pscale_ref[...]).astype(BF16)
    ext_ref[0:POOL_HALO, :] = ext_ref[tm:tm + POOL_HALO, :]


def _inproj(x2d, pprev, norm_mix, w_in_bf, qg, kg, gm, w_pool_bf, pool_scale, *, seq, pos0):
    n = x2d.shape[0]
    tm = min(seq, 512)
    tiles_per_batch = seq // tm
    row = lambda i: (i, 0)
    full2 = lambda i: (0, 0)
    out_w = lambda dt: jax.ShapeDtypeStruct((n, POOL_WIDTH), dt)
    out_h = jax.ShapeDtypeStruct((ATTN_HEADS, n, V_DIM), BF16)
    by_head = pl.BlockSpec((ATTN_HEADS, tm, V_DIM), lambda i: (0, i, 0))
    return pl.pallas_call(
        functools.partial(_inproj_kernel, tm=tm, tiles_per_batch=tiles_per_batch, pos0=pos0),
        grid=(n // tm,),
        in_specs=[
            pl.BlockSpec((tm, D_MODEL), row),
            pl.BlockSpec((1, D_MODEL), full2),
            pl.BlockSpec((D_MODEL, IN_WIDTH), full2),
            pl.BlockSpec((1, QK_WIDTH), full2),
            pl.BlockSpec((1, QK_WIDTH), full2),
            pl.BlockSpec((QK_WIDTH, QK_WIDTH), full2),
            pl.BlockSpec((1, POOL_HALO, POOL_WIDTH), lambda i: (i // tiles_per_batch, 0, 0)),
            pl.BlockSpec((len(POOL_WINDOWS), POOL_GROUP, POOL_GROUP), lambda i: (0, 0, 0)),
            pl.BlockSpec((1, POOL_WIDTH), full2),
        ],
        out_specs=[pl.BlockSpec((tm, POOL_WIDTH), row)] * 3 + [by_head, pl.BlockSpec((tm, POOL_WIDTH), row), by_head,
                                                                 pl.BlockSpec((tm, POOL_WIDTH), row)],
        out_shape=[out_w(F32), out_w(F32), out_w(F32), out_h, out_w(BF16), out_h, out_w(BF16)],
        scratch_shapes=[pltpu.VMEM((tm + POOL_HALO, POOL_WIDTH), F32)],
        compiler_params=_params(("arbitrary",)),
        name="inproj_pool",
    )(x2d, norm_mix, w_in_bf, qg, kg, gm, pprev, w_pool_bf, pool_scale)


_AUG_START = (QK_DIM, 0)


def _keyprep_kernel(k_ref, gm_ref, kp_ref, kmax_ref, *, tm):
    t = pl.program_id(1)
    k = k_ref[0].astype(F32)
    ss = jnp.dot((k * k).astype(BF16), gm_ref[...], preferred_element_type=F32)
    tile_max = jnp.max(ss, axis=0, keepdims=True)

    @pl.when(t == 0)
    def _():
        kmax_ref[0] = tile_max

    @pl.when(t > 0)
    def _():
        kmax_ref[0] = jnp.maximum(kmax_ref[0], tile_max)

    lane = lax.broadcasted_iota(jnp.int32, (tm, V_DIM), 1)
    pos = ((lax.broadcasted_iota(jnp.int32, (tm, V_DIM), 0) + t * tm) & (POS_PERIOD - 1)).astype(F32)
    for h in range(ATTN_HEADS):
        kh = k[:, h * V_DIM:(h + 1) * V_DIM]
        for sub in range(2):
            a0 = _AUG_START[sub]
            aug = jnp.where((lane >= a0) & (lane < a0 + 3), 1.0,
                            jnp.where((lane >= a0 + 3) & (lane < a0 + 6), pos, 0.0))
            own = (lane < QK_DIM) if sub == 0 else (lane >= QK_DIM)
            kp_ref[0, h, :, sub * V_DIM:(sub + 1) * V_DIM] = jnp.where(own, kh, aug).astype(BF16)


def _keyprep(k3d, gm):
    b, s, _ = k3d.shape
    tm = min(s, 512)
    return pl.pallas_call(
        functools.partial(_keyprep_kernel, tm=tm),
        grid=(b, s // tm),
        in_specs=[
            pl.BlockSpec((1, tm, QK_WIDTH), lambda bi, ti: (bi, ti, 0)),
            pl.BlockSpec((QK_WIDTH, QK_WIDTH), lambda bi, ti: (0, 0)),
        ],
        out_specs=[
            pl.BlockSpec((1, ATTN_HEADS, tm, 2 * V_DIM), lambda bi, ti: (bi, 0, ti, 0)),
            pl.BlockSpec((1, 1, QK_WIDTH), lambda bi, ti: (bi, 0, 0)),
        ],
        out_shape=[
            jax.ShapeDtypeStruct((b, ATTN_HEADS, s, 2 * V_DIM), BF16),
            jax.ShapeDtypeStruct((b, 1, QK_WIDTH), F32),
        ],
        compiler_params=_params(("arbitrary", "arbitrary")),
        name="attn_keyprep",
    )(k3d, gm)


def _split3(a):
    hi = a.astype(BF16).astype(F32)
    r = a - hi
    mid = r.astype(BF16).astype(F32)
    return hi, mid, r - mid


def _diff_finish(acc, l, lam_ref, subln_ref, tq):
    lam_v = lam_ref[...]
    lam = (jnp.exp(jnp.sum(lam_v[0:1] * lam_v[1:2], axis=1, keepdims=True))
           - jnp.exp(jnp.sum(lam_v[2:3] * lam_v[3:4], axis=1, keepdims=True)) + LAM_INIT)
    o = acc / l
    o = o[:tq] - lam * o[tq:]
    return (_rms_rows(o, subln_ref[...]) * (1.0 - LAM_INIT)).astype(BF16)


def _attn_fast_kernel(tab_ref, q_ref, kpast_ref, vpast_ref, kdiag_ref, vdiag_ref, kmax_ref, lam_ref, subln_ref,
                      o_ref, bound_ref, acc_ref, l_ref, bdiag_ref, s_ref, *, tq, tk, pos0):
    h = pl.program_id(1)
    qi = pl.program_id(2)
    slope = tab_ref[h, 0]

    @pl.when(qi == 0)
    def _():
        r = lax.rem(lax.broadcasted_iota(jnp.int32, (2 * tq, tq), 0), tq)
        c = lax.broadcasted_iota(jnp.int32, (2 * tq, tq), 1)
        allowed = (c // CHUNK) <= (r // CHUNK)
        bdiag_ref[...] = jnp.where(allowed, -slope * jnp.abs(r - c).astype(F32), NEG_INF)

    q = q_ref[0, 0].astype(F32)
    lane = lax.broadcasted_iota(jnp.int32, q.shape, 1)
    first = lane < QK_DIM
    qsq = q * q
    km = kmax_ref[0]
    bounds = (jnp.sqrt(jnp.sum(jnp.where(first, qsq, 0.0), axis=1, keepdims=True) * km[:, 0:1]),
              jnp.sqrt(jnp.sum(jnp.where(first, 0.0, qsq), axis=1, keepdims=True) * km[:, QK_DIM:QK_DIM + 1]))
    bound_ref[0, 0, 0] = jnp.broadcast_to(
        jnp.maximum(jnp.max(bounds[0], axis=0, keepdims=True), jnp.max(bounds[1], axis=0, keepdims=True)),
        (8, V_DIM))
    row = lax.broadcasted_iota(jnp.int32, (tq, 1), 0).astype(F32)

    def augmented(sub, with_alibi):
        a0 = _AUG_START[sub]
        offset = -(slope * row + bounds[sub]) if with_alibi else -bounds[sub]
        pieces = _split3(offset) + ((tab_ref[h, 1], tab_ref[h, 2], tab_ref[h, 3]) if with_alibi else ())
        aug = jnp.zeros(q.shape, F32)
        for i, piece in enumerate(pieces):
            aug = jnp.where(lane == a0 + i, piece, aug)
        own = first if sub == 0 else jnp.logical_not(first)
        return jnp.where(own, q, aug).astype(BF16)

    acc_ref[...] = jnp.zeros_like(acc_ref)
    l_ref[...] = jnp.zeros_like(l_ref)

    def consume(s, vblk):
        p = jnp.exp2(s)
        width = p.shape[1]
        if width >= V_DIM:
            part = p[:, 0:V_DIM]
            for c0 in range(V_DIM, width, V_DIM):
                part = part + p[:, c0:c0 + V_DIM]
            l_ref[...] += part
        else:
            l_ref[:, 0:width] += p
        acc_ref[...] += jnp.dot(p.astype(BF16), vblk, preferred_element_type=F32)

    base = pos0 + qi * tq
    q_past = (augmented(0, True), augmented(1, True))

    n_past = base // tk
    col = lax.broadcasted_iota(jnp.int32, (1, tk), 1)
    col_offset = slope * (col - (col & (POS_PERIOD - 1))).astype(F32)

    def past_exponents(kb):
        start = pl.multiple_of(kb * tk, tk)
        s = jnp.concatenate([_dot_nt(q_past[0], kpast_ref[0, 0, pl.ds(start, tk), 0:V_DIM]),
                             _dot_nt(q_past[1], kpast_ref[0, 0, pl.ds(start, tk), V_DIM:2 * V_DIM])], axis=0)
        return s + (col_offset - slope * (base - kb * tk).astype(F32))

    def past_values(kb):
        return vpast_ref[0, 0, pl.ds(pl.multiple_of(kb * tk, tk), tk), :]

    @pl.when(n_past > 0)
    def _():
        s_ref[...] = past_exponents(0)

    def past_block(kb, _):
        s_next = past_exponents(kb + 1)
        consume(s_ref[...], past_values(kb))
        s_ref[...] = s_next
        return 0

    lax.fori_loop(0, n_past - 1, past_block, 0)

    @pl.when(n_past > 0)
    def _():
        consume(s_ref[...], past_values(n_past - 1))

    s = jnp.concatenate([_dot_nt(augmented(0, False), kdiag_ref[0, 0, :, 0:V_DIM]),
                         _dot_nt(augmented(1, False), kdiag_ref[0, 0, :, V_DIM:2 * V_DIM])], axis=0)
    consume(s + bdiag_ref[...], vdiag_ref[0, 0])
    l = jnp.sum(l_ref[...], axis=1, keepdims=True)
    o_ref[0] = _diff_finish(acc_ref[...], l, lam_ref, subln_ref, tq)


def _attention_fast(tab, q_hm, kp_past, v_past, kp_new, v_new, kmax, lam_vecs, subln, *, pos0):
    _, b, t, _ = q_hm.shape
    s_past = kp_past.shape[2]
    tq = min(t, KEY_BLOCK)
    tk = KEY_BLOCK
    nq = t // tq
    assert pos0 % tk == 0 and t % tq == 0 and (tq == tk or nq == 1) and pos0 % CHUNK == 0 and tq % 8 == 0
    assert s_past >= pos0 + t - tq
    const2 = lambda bi, hi, qi: (0, 0)
    return pl.pallas_call(
        functools.partial(_attn_fast_kernel, tq=tq, tk=tk, pos0=pos0),
        grid=(b, ATTN_HEADS, nq),
        in_specs=[
            pl.BlockSpec(memory_space=pltpu.SMEM),
            pl.BlockSpec((1, 1, tq, V_DIM), lambda bi, hi, qi: (hi, bi, qi, 0)),
            pl.BlockSpec((1, 1, s_past, 2 * V_DIM), lambda bi, hi, qi: (bi, hi, 0, 0)),
            pl.BlockSpec((1, 1, s_past, V_DIM), lambda bi, hi, qi: (hi, bi, 0, 0)),
            pl.BlockSpec((1, 1, tq, 2 * V_DIM), lambda bi, hi, qi: (bi, hi, qi, 0)),
            pl.BlockSpec((1, 1, tq, V_DIM), lambda bi, hi, qi: (hi, bi, qi, 0)),
            pl.BlockSpec((1, 1, V_DIM), lambda bi, hi, qi: (bi, 0, hi)),
            pl.BlockSpec((4, QK_DIM), const2),
            pl.BlockSpec((1, V_DIM), const2),
        ],
        out_specs=[
            pl.BlockSpec((1, tq, V_DIM), lambda bi, hi, qi: (bi, qi, hi)),
            pl.BlockSpec((1, 1, 1, 8, V_DIM), lambda bi, hi, qi: (bi, hi, qi, 0, 0)),
        ],
        out_shape=[
            jax.ShapeDtypeStruct((b, t, ATTN_WIDTH), BF16),
            jax.ShapeDtypeStruct((b, ATTN_HEADS, nq, 8, V_DIM), F32),
        ],
        scratch_shapes=[pltpu.VMEM((2 * tq, V_DIM), F32), pltpu.VMEM((2 * tq, V_DIM), F32),
                        pltpu.VMEM((2 * tq, tq), F32), pltpu.VMEM((2 * tq, tk), F32)],
        compiler_params=_params(("arbitrary", "arbitrary", "arbitrary")),
        name="diff_attention",
    )(tab, q_hm, kp_past, v_past, kp_new, v_new, kmax, lam_vecs, subln)


def _attn_online_kernel(tab_ref, q_ref, k_ref, v_ref, lam_ref, subln_ref, o_ref, boff_ref, bdiag_ref,
                        *, tq, tk, pos0):
    h = pl.program_id(1)
    qi = pl.program_id(2)
    slope = tab_ref[h, 0]

    @pl.when(qi == 0)
    def _():
        r = lax.rem(lax.broadcasted_iota(jnp.int32, (2 * tq, tk), 0), tq)
        c = lax.broadcasted_iota(jnp.int32, (2 * tq, tk), 1)
        boff_ref[...] = -slope * (r - c).astype(F32)
        r = lax.rem(lax.broadcasted_iota(jnp.int32, (2 * tq, tq), 0), tq)
        c = lax.broadcasted_iota(jnp.int32, (2 * tq, tq), 1)
        allowed = (c // CHUNK) <= (r // CHUNK)
        bdiag_ref[...] = jnp.where(allowed, -slope * jnp.abs(r - c).astype(F32), NEG_INF)

    q = q_ref[0]
    lane = lax.broadcasted_iota(jnp.int32, q.shape, 1)
    zero = jnp.zeros_like(q)
    qs = jnp.concatenate([jnp.where(lane < QK_DIM, q, zero), jnp.where(lane >= QK_DIM, q, zero)], axis=0)
    base = pos0 + qi * tq

    def update(s, vblk, carry):
        m, l, acc = carry
        m_new = jnp.maximum(m, jnp.max(s, axis=1, keepdims=True))
        alpha = jnp.exp2(m - m_new)
        p = jnp.exp2(s - m_new)
        l = alpha * l + jnp.sum(p, axis=1, keepdims=True)
        acc = alpha * acc + jnp.dot(p.astype(BF16), vblk, preferred_element_type=F32)
        return m_new, l, acc

    def past_block(kb, carry):
        start = pl.multiple_of(kb * tk, tk)
        s = _dot_nt(qs, k_ref[0, pl.ds(start, tk), :])
        s = s + boff_ref[...] - slope * (base - kb * tk).astype(F32)
        return update(s, v_ref[0, pl.ds(start, tk), :], carry)

    carry = (jnp.full((2 * tq, 1), NEG_INF, F32), jnp.zeros((2 * tq, 1), F32), jnp.zeros((2 * tq, V_DIM), F32))
    carry = lax.fori_loop(0, base // tk, past_block, carry)
    dstart = pl.multiple_of(base, tq)
    s = _dot_nt(qs, k_ref[0, pl.ds(dstart, tq), :]) + bdiag_ref[...]
    _, l, acc = update(s, v_ref[0, pl.ds(dstart, tq), :], carry)
    o_ref[0] = _diff_finish(acc, l, lam_ref, subln_ref, tq)


def _attention_online(tab, q_bf, k_all, v_all, lam_vecs, subln, *, pos0):
    b, t, _ = q_bf.shape
    s = k_all.shape[1]
    tq = min(t, KEY_BLOCK)
    tk = KEY_BLOCK
    assert pos0 % tk == 0 and t % tq == 0 and (tq == tk or t == tq) and pos0 % CHUNK == 0 and tq % 8 == 0
    assert s == pos0 + t
    return pl.pallas_call(
        functools.partial(_attn_online_kernel, tq=tq, tk=tk, pos0=pos0),
        grid=(b, ATTN_HEADS, t // tq),
        in_specs=[
            pl.BlockSpec(memory_space=pltpu.SMEM),
            pl.BlockSpec((1, tq, V_DIM), lambda bi, hi, qi: (bi, qi, hi)),
            pl.BlockSpec((1, s, V_DIM), lambda bi, hi, qi: (bi, 0, hi)),
            pl.BlockSpec((1, s, V_DIM), lambda bi, hi, qi: (bi, 0, hi)),
            pl.BlockSpec((4, QK_DIM), lambda bi, hi, qi: (0, 0)),
            pl.BlockSpec((1, V_DIM), lambda bi, hi, qi: (0, 0)),
        ],
        out_specs=pl.BlockSpec((1, tq, V_DIM), lambda bi, hi, qi: (bi, qi, hi)),
        out_shape=jax.ShapeDtypeStruct((b, t, ATTN_WIDTH), BF16),
        scratch_shapes=[pltpu.VMEM((2 * tq, tk), F32), pltpu.VMEM((2 * tq, tq), F32)],
        compiler_params=_params(("arbitrary", "arbitrary", "arbitrary")),
        name="diff_attention_online",
    )(tab, q_bf, k_all, v_all, lam_vecs, subln)


def _alibi_table():
    cs = jnp.asarray([LOG2E * s for s in ALIBI_SLOPES], F32)
    return jnp.stack((cs,) + _split3(cs), axis=1)


def _attention(q_hm, k_prev_bf, v_prev_hm, k_bf, v_hm, gm, lam_vecs, subln, *, pos0):
    tab = _alibi_table()
    kp_new, kmax = _keyprep(k_bf, gm)
    if pos0:
        kp_past, kmax_past = _keyprep(k_prev_bf, gm)
        kmax = jnp.maximum(kmax, kmax_past)
        v_past = v_prev_hm
    else:
        kp_past, v_past = kp_new, v_hm
    att, bound = _attention_fast(tab, q_hm, kp_past, v_past, kp_new, v_hm, kmax, lam_vecs, subln, pos0=pos0)

    def online():
        rows = lambda x: jnp.transpose(x, (1, 2, 0, 3)).reshape(x.shape[1], x.shape[2], ATTN_WIDTH)
        k_all = jnp.concatenate([k_prev_bf, k_bf], axis=1) if pos0 else k_bf
        v_all = jnp.concatenate([rows(v_prev_hm), rows(v_hm)], axis=1) if pos0 else rows(v_hm)
        return _attention_online(tab, rows(q_hm), k_all, v_all, lam_vecs, subln, pos0=pos0)

    return lax.cond(jnp.max(bound) <= MAX_SAFE_BOUND, lambda: att, online)


def _outproj_kernel(x_ref, pool_ref, att_ref, wout_ref, g_ref, wq_ref, sk_ref, h_ref, hnt_ref, st_ref):
    mix = (jnp.dot(pool_ref[...], wout_ref[0:POOL_WIDTH, :], preferred_element_type=F32)
           + jnp.dot(att_ref[...], wout_ref[POOL_WIDTH:, :], preferred_element_type=F32))
    h = x_ref[...] + mix
    h_ref[...] = h
    hn = _rms_rows(h, g_ref[...])
    hnt_ref[...] = hn.T.astype(BF16)
    q = jnp.dot(hn.astype(BF16), wq_ref[...], preferred_element_type=F32)
    for g in range(2 * PEER_HEADS):
        qg = q[:, g * PEER_HALF:(g + 1) * PEER_HALF].astype(BF16)
        st_ref[g] = _dot_nt(sk_ref[g], qg)


def _outproj(x2d, pool_bf, att_bf, w_out_bf, norm_ffn, w_q_bf, sub_keys_bf):
    n = x2d.shape[0]
    tm = min(n, 256)
    row = lambda i: (i, 0)
    full2 = lambda i: (0, 0)
    return pl.pallas_call(
        _outproj_kernel,
        grid=(n // tm,),
        in_specs=[
            pl.BlockSpec((tm, D_MODEL), row),
            pl.BlockSpec((tm, POOL_WIDTH), row),
            pl.BlockSpec((tm, ATTN_WIDTH), row),
            pl.BlockSpec((D_MODEL, D_MODEL), full2),
            pl.BlockSpec((1, D_MODEL), full2),
            pl.BlockSpec((D_MODEL, 2 * PEER_HEADS * PEER_HALF), full2),
            pl.BlockSpec((2 * PEER_HEADS, PEER_KEYS, PEER_HALF), lambda i: (0, 0, 0)),
        ],
        out_specs=[
            pl.BlockSpec((tm, D_MODEL), row),
            pl.BlockSpec((D_MODEL, tm), lambda i: (0, i)),
            pl.BlockSpec((2 * PEER_HEADS, PEER_KEYS, tm), lambda i: (0, 0, i)),
        ],
        out_shape=[
            jax.ShapeDtypeStruct((n, D_MODEL), F32),
            jax.ShapeDtypeStruct((D_MODEL, n), BF16),
            jax.ShapeDtypeStruct((2 * PEER_HEADS, PEER_KEYS, n), F32),
        ],
        compiler_params=_params(("arbitrary",)),
        name="outproj_peer_query",
    )(x2d, pool_bf, att_bf, w_out_bf, norm_ffn, w_q_bf, sub_keys_bf)


def _take_max(cur, iota, exact):
    m = jnp.max(cur, axis=0, keepdims=True)
    if exact:
        idx = jnp.min(jnp.where(cur == m, iota, cur.shape[0]), axis=0, keepdims=True)
        return m, iota == idx
    return m, cur == m


def _top16_rows(s, iota_k, exact):
    tn = s.shape[1]
    row16 = lax.broadcasted_iota(jnp.int32, (PEER_TOPK, tn), 0)
    rank = jnp.full(s.shape, float(PEER_TOPK), F32)
    vals = jnp.zeros((PEER_TOPK, tn), F32)
    cur = s
    for i in range(PEER_TOPK):
        m, hit = _take_max(cur, iota_k, exact)
        rank = jnp.where(hit, float(i), rank)
        cur = jnp.where(hit, NEG_INF, cur)
        vals = jnp.where(row16 == i, m, vals)
    return vals, rank


_RANK_SUM = float(sum(range(PEER_TOPK)) + PEER_TOPK * (PEER_KEYS - PEER_TOPK))


def _topk_kernel(s_ref, cnt_ref, ea_ref, rank_ref, eb_ref, *, tn):
    surplus = _topk_heads(s_ref, cnt_ref, ea_ref, rank_ref, eb_ref, tn, exact=False)

    @pl.when(jnp.max(surplus) > 0.0)
    def _():
        _topk_heads(s_ref, cnt_ref, ea_ref, rank_ref, eb_ref, tn, exact=True)


def _topk_heads(s_ref, cnt_ref, ea_ref, rank_ref, eb_ref, tn, exact):
    iota_k = lax.broadcasted_iota(jnp.int32, (PEER_KEYS, tn), 0)
    iota_c = lax.broadcasted_iota(jnp.int32, (_CAND_ROWS, tn), 0)
    row8 = lax.broadcasted_iota(jnp.int32, (8, tn), 0)
    surplus = jnp.zeros((1, tn), F32)
    for h in range(PEER_HEADS):
        sa = s_ref[2 * h]
        sb = s_ref[2 * h + 1]
        va, rank_a = _top16_rows(sa, iota_k, exact)
        vb, rank_b = _top16_rows(sb, iota_k, exact)
        pieces = []
        for i0, ni, nj in _CAND_PIECES:
            if ni == 1:
                rows = 16 if nj > 8 else 8
                piece = va[i0:i0 + 1] + vb[0:rows]
                if nj < rows:
                    piece = jnp.where(row8 < nj, piece, NEG_INF)
            else:
                piece = va[i0:i0 + ni] + vb[0:1]
            pieces.append(piece)
        cand = jnp.concatenate(pieces, axis=0)
        cur = cand
        sel = jnp.zeros(cand.shape, F32)
        for _ in range(PEER_TOPK):
            _, hit = _take_max(cur, iota_c, exact)
            sel = jnp.where(hit, 1.0, sel)
            cur = jnp.where(hit, NEG_INF, cur)
        taken = (jnp.sum(rank_a, axis=0, keepdims=True) + jnp.sum(rank_b, axis=0, keepdims=True)
                 - jnp.sum(sel, axis=0, keepdims=True))
        surplus = jnp.maximum(surplus, jnp.abs(taken - (2.0 * _RANK_SUM - PEER_TOPK)))
        z = jnp.sum(jnp.where(sel > 0.0, jnp.exp(cand - cand[0:1]), 0.0), axis=0, keepdims=True)
        cnt = jnp.zeros(sa.shape, F32)
        off = 0
        for i0, ni, nj in _CAND_PIECES:
            rows = (16 if nj > 8 else 8) if ni == 1 else ni
            if ni == 1:
                c_i = jnp.sum(sel[off:off + rows], axis=0, keepdims=True)
                cnt = jnp.where(rank_a == float(i0), c_i, cnt)
            else:
                for r in range(ni):
                    cnt = jnp.where(rank_a == float(i0 + r), sel[off + r:off + r + 1], cnt)
            off += rows
        cnt_ref[h] = cnt
        ea_ref[h] = jnp.exp(sa - va[0:1])
        rank_ref[h] = rank_b.astype(BF16)
        eb_ref[h] = (jnp.exp(sb - vb[0:1]) / z).astype(BF16)
    return surplus


def _topk(scores_t):
    n = scores_t.shape[2]
    tn = min(n, 128)
    spec_in = pl.BlockSpec((2 * PEER_HEADS, PEER_KEYS, tn), lambda i: (0, 0, i))
    spec_out = pl.BlockSpec((PEER_HEADS, PEER_KEYS, tn), lambda i: (0, 0, i))
    shape_out = lambda dt: jax.ShapeDtypeStruct((PEER_HEADS, PEER_KEYS, n), dt)
    return pl.pallas_call(
        functools.partial(_topk_kernel, tn=tn),
        grid=(n // tn,),
        in_specs=[spec_in],
        out_specs=[spec_out] * 4,
        out_shape=[shape_out(F32), shape_out(F32), shape_out(BF16), shape_out(BF16)],
        compiler_params=_params(("arbitrary",)),
        name="peer_topk",
    )(scores_t)


def _peer_kernel(hnt_ref, h_ref, u_ref, vta_ref, vtb_ref, cnt_ref, ea_ref, rank_ref, eb_ref, y_ref,
                 acc_ref, acta_ref, actb_ref, *, te):
    step = pl.program_id(1)
    last = pl.num_programs(1) - 1

    @pl.when(step == 0)
    def _():
        acc_ref[...] = jnp.zeros_like(acc_ref)
        actb_ref[...] = jnp.zeros_like(actb_ref)

    hnt = hnt_ref[...]
    tn = hnt.shape[1]
    pack = 16
    tiles = PEER_KEYS // pack
    zero = jnp.zeros((tiles, pack, tn), BF16)

    def key_row(ref, h, a_glob):
        return jnp.broadcast_to(ref[h, pl.ds(a_glob, 1), :], (pack, tn)).astype(BF16)[None]

    def contribution(act_ref, vt_ref, block):
        coefs = []
        for r0 in range(0, te, PEER_KEYS):
            a_glob = block * (te // PEER_KEYS) + r0 // PEER_KEYS
            gate = None
            for h in range(PEER_HEADS):
                cnt = key_row(cnt_ref, h, a_glob)
                rank = rank_ref[h].reshape(tiles, pack, tn)
                term = jnp.where(rank < cnt, eb_ref[h].reshape(tiles, pack, tn), zero) * key_row(ea_ref, h, a_glob)
                gate = term if gate is None else gate + term
            x = act_ref[r0:r0 + PEER_KEYS, :]
            gelu = 0.5 * x * (1.0 + lax.erf(x * (2.0 ** -0.5)))
            coefs.append(gate.reshape(PEER_KEYS, tn) * gelu.astype(BF16))
        return jnp.dot(vt_ref[0], jnp.concatenate(coefs, axis=0), preferred_element_type=F32)

    acta_ref[...] = jnp.dot(u_ref[0:te, :], hnt, preferred_element_type=F32)
    part_b = contribution(actb_ref, vtb_ref, jnp.maximum(2 * step - 1, 0))
    actb_ref[...] = jnp.dot(u_ref[te:2 * te, :], hnt, preferred_element_type=F32)
    part_a = contribution(acta_ref, vta_ref, jnp.minimum(2 * step, 2 * last - 1))
    acc_ref[...] += jnp.where(step > 0, part_b, 0.0) + jnp.where(step < last, part_a, 0.0)

    @pl.when(step == last)
    def _():
        y_ref[...] = h_ref[...] + acc_ref[...].T


def _peer(hn_t, h2d, u_bf, vt_bf, cnt, ea, rank, eb):
    n = h2d.shape[0]
    tn = min(n, 512)
    te = PEER_BLOCK
    n_pairs = PEER_EXPERTS // (2 * te)
    tok = lambda i, s: (0, 0, i)
    fac = pl.BlockSpec((PEER_HEADS, PEER_KEYS, tn), tok)
    return pl.pallas_call(
        functools.partial(_peer_kernel, te=te),
        grid=(n // tn, n_pairs + 1),
        in_specs=[
            pl.BlockSpec((D_MODEL, tn), lambda i, s: (0, i)),
            pl.BlockSpec((tn, D_MODEL), lambda i, s: (i, 0)),
            pl.BlockSpec((2 * te, D_MODEL), lambda i, s: (jnp.minimum(s, n_pairs - 1), 0)),
            pl.BlockSpec((1, D_MODEL, te), lambda i, s: (jnp.minimum(2 * s, 2 * n_pairs - 1), 0, 0)),
            pl.BlockSpec((1, D_MODEL, te), lambda i, s: (jnp.maximum(2 * s - 1, 0), 0, 0)),
            fac, fac, fac, fac,
        ],
        out_specs=pl.BlockSpec((tn, D_MODEL), lambda i, s: (i, 0)),
        out_shape=jax.ShapeDtypeStruct((n, D_MODEL), F32),
        scratch_shapes=[pltpu.VMEM((D_MODEL, tn), F32), pltpu.VMEM((te, tn), F32), pltpu.VMEM((te, tn), F32)],
        compiler_params=_params(("arbitrary", "arbitrary")),
        name="peer_dense",
    )(hn_t, h2d, u_bf, vt_bf, vt_bf, cnt, ea, rank, eb)


def _stream(x, pool_prev, k_prev, v_prev, w, lam_vecs):
    b, t, _ = x.shape
    past = k_prev.shape[1]
    n = b * t
    x2d = x.reshape(n, D_MODEL)
    pprev = jnp.concatenate([jnp.zeros((b, POOL_HALO - POOL_TAIL, POOL_WIDTH), F32), pool_prev], axis=1)
    p, k_store, v, q_bf, k_bf, v_bf, pool_bf = _inproj(
        x2d, pprev, w["norm_mix"], w["w_in"], w["q_gain"], w["k_gain"], w["group_ones"],
        w["w_pool"], w["pool_scale"], seq=t, pos0=past)
    att = _attention(q_bf.reshape(ATTN_HEADS, b, t, V_DIM),
                     k_prev.reshape(b, past, QK_WIDTH).astype(BF16),
                     jnp.transpose(v_prev, (2, 0, 1, 3)).astype(BF16),
                     k_bf.reshape(b, t, QK_WIDTH), v_bf.reshape(ATTN_HEADS, b, t, V_DIM),
                     w["group_ones"], lam_vecs, w["subln"], pos0=past)
    h2d, hn_t, scores_t = _outproj(x2d, pool_bf, att.reshape(n, ATTN_WIDTH), w["w_out"], w["norm_ffn"],
                                   w["w_peer_q"], w["sub_keys"])
    cnt, ea, rank, eb = _topk(scores_t)
    y = _peer(hn_t, h2d, w["peer_u"], w["peer_vt"], cnt, ea, rank, eb)
    tail = jnp.concatenate([pool_prev, p.reshape(b, t, POOL_WIDTH)], axis=1)[:, -POOL_TAIL:]
    return (y.reshape(b, t, D_MODEL), k_store.reshape(b, t, ATTN_HEADS, 2 * QK_DIM),
            v.reshape(b, t, ATTN_HEADS, V_DIM), tail)


def kernel(x_prompt, x_sample, cache_k, cache_v, state_pool, norm_mix, w_in, q_norm, k_norm, lambda_q1, lambda_k1, lambda_q2, lambda_k2, subln, w_pool, pool_scale, w_out, norm_ffn, w_peer_q, peer_sub_keys, peer_u, peer_v):
    depth = w_in.shape[0]
    assert depth == 1
    l = 0
    group = jnp.arange(QK_WIDTH) // QK_DIM
    w = {
        "norm_mix": norm_mix[l][None, :],
        "w_in": w_in[l].astype(BF16),
        "q_gain": jnp.tile(q_norm[l], QK_WIDTH // QK_DIM)[None, :],
        "k_gain": jnp.tile(k_norm[l], QK_WIDTH // QK_DIM)[None, :],
        "group_ones": (group[:, None] == group[None, :]).astype(BF16),
        "w_pool": w_pool[l].astype(BF16),
        "pool_scale": pool_scale[l][None, :],
        "subln": subln[l][None, :],
        "w_out": w_out[l].astype(BF16),
        "norm_ffn": norm_ffn[l][None, :],
        "w_peer_q": w_peer_q[l].astype(BF16),
        "sub_keys": peer_sub_keys[l].reshape(2 * PEER_HEADS, PEER_KEYS, PEER_HALF).astype(BF16),
        "peer_u": peer_u[l].astype(BF16),
        "peer_vt": jnp.transpose(peer_v[l].reshape(PEER_EXPERTS // PEER_BLOCK, PEER_BLOCK, D_MODEL),
                                 (0, 2, 1)).astype(BF16),
    }
    lam_vecs = jnp.stack([lambda_q1[l], lambda_k1[l], lambda_q2[l], lambda_k2[l]]).astype(F32)
    bp = x_prompt.shape[0]
    zero_pool = jnp.zeros((bp, POOL_TAIL, POOL_WIDTH), F32)
    zero_k = jnp.zeros((bp, 0, ATTN_HEADS, 2 * QK_DIM), F32)
    zero_v = jnp.zeros((bp, 0, ATTN_HEADS, V_DIM), F32)
    yp, kp, vp, pp = _stream(x_prompt, zero_pool, zero_k, zero_v, w, lam_vecs)
    ys, kn, vn, pn = _stream(x_sample, state_pool[l], cache_k[l], cache_v[l], w, lam_vecs)
    return (yp, ys, kp[None], vp[None], pp[None], kn[None], vn[None], pn[None])
```

```python
import functools
import math

import jax
import jax.numpy as jnp
from jax import lax
from jax.experimental import pallas as pl
from jax.experimental.pallas import tpu as pltpu

F32 = jnp.float32
BF16 = jnp.bfloat16

D_MODEL = 1024
CHUNK = 64
POOL_WIDTH = 512
POOL_WINDOWS = (2, 4, 8, 16)
POOL_GROUP = 128
POOL_TAIL = 15
POOL_HALO = 16
ATTN_HEADS = 4
QK_DIM = 64
V_DIM = 128
QK_WIDTH = 512
ATTN_WIDTH = 512
IN_WIDTH = 2048
ALIBI_SLOPES = tuple(2.0 ** (-8.0 * (h + 1) / ATTN_HEADS) for h in range(ATTN_HEADS))
PEER_HEADS = 8
PEER_KEYS = 128
PEER_EXPERTS = PEER_KEYS * PEER_KEYS
PEER_HALF = 128
PEER_TOPK = 16
PEER_BLOCK = 1024
NORM_EPS = 1e-6
NEG_INF = -1e30
LAM_INIT = 0.8 - 0.6 * math.exp(-0.3 * 0)
LOG2E = 1.4426950408889634
KEY_BLOCK = 512
POS_PERIOD = 256
MAX_SAFE_BOUND = 60.0

VMEM_LIMIT_BYTES = 56 * 1024 * 1024

_CAND_PIECES = ((0, 1, 16), (1, 1, 8), (2, 1, 5), (3, 1, 4), (4, 1, 3), (5, 1, 2), (6, 1, 2), (7, 1, 2), (8, 8, 1))
_CAND_ROWS = 80


def _params(semantics):
    return pltpu.CompilerParams(dimension_semantics=semantics, vmem_limit_bytes=VMEM_LIMIT_BYTES)


def _rms_rows(x, gain):
    ms = jnp.mean(x * x, axis=-1, keepdims=True)
    return x * lax.rsqrt(ms + NORM_EPS) * gain


def _dot_nt(a, b):
    return lax.dot_general(a, b, (((1,), (1,)), ((), ())), preferred_element_type=F32)


_AUG_START = (QK_DIM, 0)


def _store_aug_keys(kp_ref, k, row0):
    tm = k.shape[0]
    lane = lax.broadcasted_iota(jnp.int32, (tm, V_DIM), 1)
    pos = ((lax.broadcasted_iota(jnp.int32, (tm, V_DIM), 0) + row0) & (POS_PERIOD - 1)).astype(F32)
    for h in range(ATTN_HEADS):
        kh = k[:, h * V_DIM:(h + 1) * V_DIM]
        for sub in range(2):
            a0 = _AUG_START[sub]
            aug = jnp.where((lane >= a0) & (lane < a0 + 3), 1.0,
                            jnp.where((lane >= a0 + 3) & (lane < a0 + 6), pos, 0.0))
            own = (lane < QK_DIM) if sub == 0 else (lane >= QK_DIM)
            kp_ref[0, h, :, sub * V_DIM:(sub + 1) * V_DIM] = jnp.where(own, kh, aug).astype(BF16)


def _running_max(ref, first, tile_max):
    @pl.when(first)
    def _():
        ref[0] = tile_max

    @pl.when(jnp.logical_not(first))
    def _():
        ref[0] = jnp.maximum(ref[0], tile_max)


def _inproj_kernel(x_ref, g_ref, win_ref, qg_ref, kg_ref, kg2_ref, gm_ref, pprev_ref, wpool_ref, pscale_ref,
                   p_ref, ks_ref, v_ref, qb_ref, kp_ref, kmax_ref, vb_ref, pool_ref, ext_ref,
                   *, tm, tiles_per_batch, pos0):
    t_in_batch = lax.rem(pl.program_id(0), tiles_per_batch)
    hn = _rms_rows(x_ref[...], g_ref[...]).astype(BF16)
    proj = jnp.dot(hn, win_ref[...], preferred_element_type=F32)
    p = proj[:, :POOL_WIDTH]
    q = proj[:, POOL_WIDTH:POOL_WIDTH + QK_WIDTH]
    k = proj[:, POOL_WIDTH + QK_WIDTH:POOL_WIDTH + 2 * QK_WIDTH]
    v = proj[:, POOL_WIDTH + 2 * QK_WIDTH:]

    def group_rms(z, gain):
        sq = z * z
        hi = sq.astype(BF16)
        lo = (sq - hi.astype(F32)).astype(BF16)
        ss = (jnp.dot(hi, gm_ref[...], preferred_element_type=F32)
              + jnp.dot(lo, gm_ref[...], preferred_element_type=F32))
        inv = lax.rsqrt(ss * (1.0 / QK_DIM) + NORM_EPS)
        return z * inv * gain, ss * inv * inv

    qn, _ = group_rms(q, qg_ref[...])
    kn, k_unit = group_rms(k, kg_ref[...])
    p_ref[...] = p
    for h in range(ATTN_HEADS):
        ks_ref[pl.ds(h, tm, stride=ATTN_HEADS), :] = kn[:, h * V_DIM:(h + 1) * V_DIM]
        v_ref[pl.ds(h, tm, stride=ATTN_HEADS), :] = v[:, h * V_DIM:(h + 1) * V_DIM]
    qb = (qn * (QK_DIM ** -0.5 * LOG2E)).astype(BF16)
    vb = v.astype(BF16)
    for h in range(ATTN_HEADS):
        qb_ref[h] = qb[:, h * V_DIM:(h + 1) * V_DIM]
        vb_ref[h] = vb[:, h * V_DIM:(h + 1) * V_DIM]
    _store_aug_keys(kp_ref, kn, pos0 + t_in_batch * tm)
    _running_max(kmax_ref, t_in_batch == 0, jnp.max(k_unit * kg2_ref[...], axis=0, keepdims=True))

    @pl.when(t_in_batch == 0)
    def _():
        ext_ref[0:POOL_HALO, :] = pprev_ref[0]

    ext_ref[POOL_HALO:POOL_HALO + tm, :] = p
    pos = lax.broadcasted_iota(jnp.int32, (tm, POOL_GROUP), 0) + (pos0 + t_in_batch * tm)
    mixed = []
    for g, w in enumerate(POOL_WINDOWS):
        lanes = slice(g * POOL_GROUP, (g + 1) * POOL_GROUP)
        win = ext_ref[POOL_HALO:POOL_HALO + tm, lanes]
        for j in range(1, w):
            win = win + ext_ref[POOL_HALO - j:POOL_HALO - j + tm, lanes]
        cnt = jnp.minimum(w, pos + 1).astype(F32)
        pooled = win / cnt - p[:, lanes]
        mixed.append(jnp.dot(pooled.astype(BF16), wpool_ref[g], preferred_element_type=F32))
    pool_ref[...] = (jnp.concatenate(mixed, axis=1) * pscale_ref[...]).astype(BF16)
    ext_ref[0:POOL_HALO, :] = ext_ref[tm:tm + POOL_HALO, :]


def _inproj(x2d, pprev, norm_mix, w_in_bf, qg, kg, kg2, gm, w_pool_bf, pool_scale, *, seq, pos0):
    n = x2d.shape[0]
    b = n // seq
    tm = min(seq, 512)
    tiles_per_batch = seq // tm
    row = lambda i: (i, 0)
    full2 = lambda i: (0, 0)
    rows_spec = pl.BlockSpec((tm, POOL_WIDTH), row)
    out_w = lambda dt: jax.ShapeDtypeStruct((n, POOL_WIDTH), dt)
    out_h = jax.ShapeDtypeStruct((ATTN_HEADS, n, V_DIM), BF16)
    by_head = pl.BlockSpec((ATTN_HEADS, tm, V_DIM), lambda i: (0, i, 0))
    out_cache = jax.ShapeDtypeStruct((n * ATTN_HEADS, V_DIM), F32)
    cache_spec = pl.BlockSpec((tm * ATTN_HEADS, V_DIM), row)
    return pl.pallas_call(
        functools.partial(_inproj_kernel, tm=tm, tiles_per_batch=tiles_per_batch, pos0=pos0),
        grid=(n // tm,),
        in_specs=[
            pl.BlockSpec((tm, D_MODEL), row),
            pl.BlockSpec((1, D_MODEL), full2),
            pl.BlockSpec((D_MODEL, IN_WIDTH), full2),
            pl.BlockSpec((1, QK_WIDTH), full2),
            pl.BlockSpec((1, QK_WIDTH), full2),
            pl.BlockSpec((1, QK_WIDTH), full2),
            pl.BlockSpec((QK_WIDTH, QK_WIDTH), full2),
            pl.BlockSpec((1, POOL_HALO, POOL_WIDTH), lambda i: (i // tiles_per_batch, 0, 0)),
            pl.BlockSpec((len(POOL_WINDOWS), POOL_GROUP, POOL_GROUP), lambda i: (0, 0, 0)),
            pl.BlockSpec((1, POOL_WIDTH), full2),
        ],
        out_specs=[
            rows_spec, cache_spec, cache_spec, by_head,
            pl.BlockSpec((1, ATTN_HEADS, tm, 2 * V_DIM),
                         lambda i: (i // tiles_per_batch, 0, lax.rem(i, tiles_per_batch), 0)),
            pl.BlockSpec((1, 1, QK_WIDTH), lambda i: (i // tiles_per_batch, 0, 0)),
            by_head, rows_spec,
        ],
        out_shape=[out_w(F32), out_cache, out_cache, out_h,
                   jax.ShapeDtypeStruct((b, ATTN_HEADS, seq, 2 * V_DIM), BF16),
                   jax.ShapeDtypeStruct((b, 1, QK_WIDTH), F32), out_h, out_w(BF16)],
        scratch_shapes=[pltpu.VMEM((tm + POOL_HALO, POOL_WIDTH), F32)],
        compiler_params=_params(("arbitrary",)),
        name="inproj_pool",
    )(x2d, norm_mix, w_in_bf, qg, kg, kg2, gm, pprev, w_pool_bf, pool_scale)


def _keyprep_kernel(k_ref, gm_ref, kp_ref, kmax_ref, *, tm):
    t = pl.program_id(1)
    k = k_ref[0].astype(F32)
    ss = jnp.dot((k * k).astype(BF16), gm_ref[...], preferred_element_type=F32)
    _running_max(kmax_ref, t == 0, jnp.max(ss, axis=0, keepdims=True))
    _store_aug_keys(kp_ref, k, t * tm)


def _keyprep(k3d, gm):
    b, s, _ = k3d.shape
    tm = min(s, 512)
    return pl.pallas_call(
        functools.partial(_keyprep_kernel, tm=tm),
        grid=(b, s // tm),
        in_specs=[
            pl.BlockSpec((1, tm, QK_WIDTH), lambda bi, ti: (bi, ti, 0)),
            pl.BlockSpec((QK_WIDTH, QK_WIDTH), lambda bi, ti: (0, 0)),
        ],
        out_specs=[
            pl.BlockSpec((1, ATTN_HEADS, tm, 2 * V_DIM), lambda bi, ti: (bi, 0, ti, 0)),
            pl.BlockSpec((1, 1, QK_WIDTH), lambda bi, ti: (bi, 0, 0)),
        ],
        out_shape=[
            jax.ShapeDtypeStruct((b, ATTN_HEADS, s, 2 * V_DIM), BF16),
            jax.ShapeDtypeStruct((b, 1, QK_WIDTH), F32),
        ],
        compiler_params=_params(("arbitrary", "arbitrary")),
        name="attn_keyprep",
    )(k3d, gm)


def _split3(a):
    hi = a.astype(BF16).astype(F32)
    r = a - hi
    mid = r.astype(BF16).astype(F32)
    return hi, mid, r - mid


def _diff_finish(acc, l, lam_ref, subln_ref, tq):
    lam_v = lam_ref[...]
    lam = (jnp.exp(jnp.sum(lam_v[0:1] * lam_v[1:2], axis=1, keepdims=True))
           - jnp.exp(jnp.sum(lam_v[2:3] * lam_v[3:4], axis=1, keepdims=True)) + LAM_INIT)
    o = acc / l
    o = o[:tq] - lam * o[tq:]
    return (_rms_rows(o, subln_ref[...]) * (1.0 - LAM_INIT)).astype(BF16)


def _attn_fast_kernel(tab_ref, q_ref, kpast_ref, vpast_ref, kdiag_ref, vdiag_ref, kmax_ref, lam_ref, subln_ref,
                      o_ref, bound_ref, acc_ref, l_ref, bdiag_ref, s_ref, *, tq, tk, pos0):
    h = pl.program_id(1)
    qi = pl.program_id(2)
    slope = tab_ref[h, 0]

    @pl.when(qi == 0)
    def _():
        r = lax.rem(lax.broadcasted_iota(jnp.int32, (2 * tq, tq), 0), tq)
        c = lax.broadcasted_iota(jnp.int32, (2 * tq, tq), 1)
        allowed = (c // CHUNK) <= (r // CHUNK)
        bdiag_ref[...] = jnp.where(allowed, -slope * jnp.abs(r - c).astype(F32), NEG_INF)

    q = q_ref[0, 0].astype(F32)
    lane = lax.broadcasted_iota(jnp.int32, q.shape, 1)
    first = lane < QK_DIM
    qsq = q * q
    km = kmax_ref[0]
    bounds = (jnp.sqrt(jnp.sum(jnp.where(first, qsq, 0.0), axis=1, keepdims=True) * km[:, 0:1]),
              jnp.sqrt(jnp.sum(jnp.where(first, 0.0, qsq), axis=1, keepdims=True) * km[:, QK_DIM:QK_DIM + 1]))
    bound_ref[0, 0, 0] = jnp.broadcast_to(
        jnp.maximum(jnp.max(bounds[0], axis=0, keepdims=True), jnp.max(bounds[1], axis=0, keepdims=True)),
        (8, V_DIM))
    row = lax.broadcasted_iota(jnp.int32, (tq, 1), 0).astype(F32)

    def augmented(sub, with_alibi):
        a0 = _AUG_START[sub]
        offset = -(slope * row + bounds[sub]) if with_alibi else -bounds[sub]
        pieces = _split3(offset) + ((tab_ref[h, 1], tab_ref[h, 2], tab_ref[h, 3]) if with_alibi else ())
        aug = jnp.zeros(q.shape, F32)
        for i, piece in enumerate(pieces):
            aug = jnp.where(lane == a0 + i, piece, aug)
        own = first if sub == 0 else jnp.logical_not(first)
        return jnp.where(own, q, aug).astype(BF16)

    acc_ref[...] = jnp.zeros_like(acc_ref)
    l_ref[...] = jnp.zeros_like(l_ref)

    def consume(s, vblk):
        p = jnp.exp2(s)
        width = p.shape[1]
        if width >= V_DIM:
            part = p[:, 0:V_DIM]
            for c0 in range(V_DIM, width, V_DIM):
                part = part + p[:, c0:c0 + V_DIM]
            l_ref[...] += part
        else:
            l_ref[:, 0:width] += p
        acc_ref[...] += jnp.dot(p.astype(BF16), vblk, preferred_element_type=F32)

    base = pos0 + qi * tq
    q_past = (augmented(0, True), augmented(1, True))

    n_past = base // tk
    n_wide = n_past // 2
    col = lax.broadcasted_iota(jnp.int32, (1, 2 * tk), 1)
    col_offset = slope * (col - (col & (POS_PERIOD - 1))).astype(F32)

    def past_exponents(start, width):
        start = start if isinstance(start, int) else pl.multiple_of(start, tk)
        s = jnp.concatenate([_dot_nt(q_past[0], kpast_ref[0, 0, pl.ds(start, width), 0:V_DIM]),
                             _dot_nt(q_past[1], kpast_ref[0, 0, pl.ds(start, width), V_DIM:2 * V_DIM])], axis=0)
        return s + (col_offset[:, 0:width] - slope * (base - start).astype(F32))

    def past_values(start, width):
        start = start if isinstance(start, int) else pl.multiple_of(start, tk)
        return vpast_ref[0, 0, pl.ds(start, width), :]

    @pl.when(n_wide > 0)
    def _():
        s_ref[...] = past_exponents(0, 2 * tk)

    def past_block(j, _):
        s_next = past_exponents((j + 1) * (2 * tk), 2 * tk)
        consume(s_ref[...], past_values(j * (2 * tk), 2 * tk))
        s_ref[...] = s_next
        return 0

    lax.fori_loop(0, n_wide - 1, past_block, 0)

    @pl.when(n_wide > 0)
    def _():
        consume(s_ref[...], past_values((n_wide - 1) * (2 * tk), 2 * tk))

    @pl.when(n_past > 2 * n_wide)
    def _():
        start = (n_past - 1) * tk
        consume(past_exponents(start, tk), past_values(start, tk))

    s = jnp.concatenate([_dot_nt(augmented(0, False), kdiag_ref[0, 0, :, 0:V_DIM]),
                         _dot_nt(augmented(1, False), kdiag_ref[0, 0, :, V_DIM:2 * V_DIM])], axis=0)
    consume(s + bdiag_ref[...], vdiag_ref[0, 0])
    l = jnp.sum(l_ref[...], axis=1, keepdims=True)
    o_ref[0] = _diff_finish(acc_ref[...], l, lam_ref, subln_ref, tq)


def _attention_fast(tab, q_hm, kp_past, v_past, kp_new, v_new, kmax, lam_vecs, subln, *, pos0):
    _, b, t, _ = q_hm.shape
    s_past = kp_past.shape[2]
    tq = min(t, KEY_BLOCK)
    tk = KEY_BLOCK
    nq = t // tq
    assert pos0 % tk == 0 and t % tq == 0 and (tq == tk or nq == 1) and pos0 % CHUNK == 0 and tq % 8 == 0
    assert s_past >= pos0 + t - tq
    const2 = lambda bi, hi, qi: (0, 0)
    return pl.pallas_call(
        functools.partial(_attn_fast_kernel, tq=tq, tk=tk, pos0=pos0),
        grid=(b, ATTN_HEADS, nq),
        in_specs=[
            pl.BlockSpec(memory_space=pltpu.SMEM),
            pl.BlockSpec((1, 1, tq, V_DIM), lambda bi, hi, qi: (hi, bi, qi, 0)),
            pl.BlockSpec((1, 1, s_past, 2 * V_DIM), lambda bi, hi, qi: (bi, hi, 0, 0)),
            pl.BlockSpec((1, 1, s_past, V_DIM), lambda bi, hi, qi: (hi, bi, 0, 0)),
            pl.BlockSpec((1, 1, tq, 2 * V_DIM), lambda bi, hi, qi: (bi, hi, qi, 0)),
            pl.BlockSpec((1, 1, tq, V_DIM), lambda bi, hi, qi: (hi, bi, qi, 0)),
            pl.BlockSpec((1, 1, V_DIM), lambda bi, hi, qi: (bi, 0, hi)),
            pl.BlockSpec((4, QK_DIM), const2),
            pl.BlockSpec((1, V_DIM), const2),
        ],
        out_specs=[
            pl.BlockSpec((1, tq, V_DIM), lambda bi, hi, qi: (bi, qi, hi)),
            pl.BlockSpec((1, 1, 1, 8, V_DIM), lambda bi, hi, qi: (bi, hi, qi, 0, 0)),
        ],
        out_shape=[
            jax.ShapeDtypeStruct((b, t, ATTN_WIDTH), BF16),
            jax.ShapeDtypeStruct((b, ATTN_HEADS, nq, 8, V_DIM), F32),
        ],
        scratch_shapes=[pltpu.VMEM((2 * tq, V_DIM), F32), pltpu.VMEM((2 * tq, V_DIM), F32),
                        pltpu.VMEM((2 * tq, tq), F32), pltpu.VMEM((2 * tq, 2 * tk), F32)],
        compiler_params=_params(("arbitrary", "arbitrary", "arbitrary")),
        name="diff_attention",
    )(tab, q_hm, kp_past, v_past, kp_new, v_new, kmax, lam_vecs, subln)


def _attn_online_kernel(tab_ref, q_ref, k_ref, v_ref, lam_ref, subln_ref, o_ref, boff_ref, bdiag_ref,
                        *, tq, tk, pos0):
    h = pl.program_id(1)
    qi = pl.program_id(2)
    slope = tab_ref[h, 0]

    @pl.when(qi == 0)
    def _():
        r = lax.rem(lax.broadcasted_iota(jnp.int32, (2 * tq, tk), 0), tq)
        c = lax.broadcasted_iota(jnp.int32, (2 * tq, tk), 1)
        boff_ref[...] = -slope * (r - c).astype(F32)
        r = lax.rem(lax.broadcasted_iota(jnp.int32, (2 * tq, tq), 0), tq)
        c = lax.broadcasted_iota(jnp.int32, (2 * tq, tq), 1)
        allowed = (c // CHUNK) <= (r // CHUNK)
        bdiag_ref[...] = jnp.where(allowed, -slope * jnp.abs(r - c).astype(F32), NEG_INF)

    q = q_ref[0]
    lane = lax.broadcasted_iota(jnp.int32, q.shape, 1)
    zero = jnp.zeros_like(q)
    qs = jnp.concatenate([jnp.where(lane < QK_DIM, q, zero), jnp.where(lane >= QK_DIM, q, zero)], axis=0)
    base = pos0 + qi * tq

    def update(s, vblk, carry):
        m, l, acc = carry
        m_new = jnp.maximum(m, jnp.max(s, axis=1, keepdims=True))
        alpha = jnp.exp2(m - m_new)
        p = jnp.exp2(s - m_new)
        l = alpha * l + jnp.sum(p, axis=1, keepdims=True)
        acc = alpha * acc + jnp.dot(p.astype(BF16), vblk, preferred_element_type=F32)
        return m_new, l, acc

    def past_block(kb, carry):
        start = pl.multiple_of(kb * tk, tk)
        s = _dot_nt(qs, k_ref[0, pl.ds(start, tk), :])
        s = s + boff_ref[...] - slope * (base - kb * tk).astype(F32)
        return update(s, v_ref[0, pl.ds(start, tk), :], carry)

    carry = (jnp.full((2 * tq, 1), NEG_INF, F32), jnp.zeros((2 * tq, 1), F32), jnp.zeros((2 * tq, V_DIM), F32))
    carry = lax.fori_loop(0, base // tk, past_block, carry)
    dstart = pl.multiple_of(base, tq)
    s = _dot_nt(qs, k_ref[0, pl.ds(dstart, tq), :]) + bdiag_ref[...]
    _, l, acc = update(s, v_ref[0, pl.ds(dstart, tq), :], carry)
    o_ref[0] = _diff_finish(acc, l, lam_ref, subln_ref, tq)


def _attention_online(tab, q_bf, k_all, v_all, lam_vecs, subln, *, pos0):
    b, t, _ = q_bf.shape
    s = k_all.shape[1]
    tq = min(t, KEY_BLOCK)
    tk = KEY_BLOCK
    assert pos0 % tk == 0 and t % tq == 0 and (tq == tk or t == tq) and pos0 % CHUNK == 0 and tq % 8 == 0
    assert s == pos0 + t
    return pl.pallas_call(
        functools.partial(_attn_online_kernel, tq=tq, tk=tk, pos0=pos0),
        grid=(b, ATTN_HEADS, t // tq),
        in_specs=[
            pl.BlockSpec(memory_space=pltpu.SMEM),
            pl.BlockSpec((1, tq, V_DIM), lambda bi, hi, qi: (bi, qi, hi)),
            pl.BlockSpec((1, s, V_DIM), lambda bi, hi, qi: (bi, 0, hi)),
            pl.BlockSpec((1, s, V_DIM), lambda bi, hi, qi: (bi, 0, hi)),
            pl.BlockSpec((4, QK_DIM), lambda bi, hi, qi: (0, 0)),
            pl.BlockSpec((1, V_DIM), lambda bi, hi, qi: (0, 0)),
        ],
        out_specs=pl.BlockSpec((1, tq, V_DIM), lambda bi, hi, qi: (bi, qi, hi)),
        out_shape=jax.ShapeDtypeStruct((b, t, ATTN_WIDTH), BF16),
        scratch_shapes=[pltpu.VMEM((2 * tq, tk), F32), pltpu.VMEM((2 * tq, tq), F32)],
        compiler_params=_params(("arbitrary", "arbitrary", "arbitrary")),
        name="diff_attention_online",
    )(tab, q_bf, k_all, v_all, lam_vecs, subln)


def _alibi_table():
    cs = jnp.asarray([LOG2E * s for s in ALIBI_SLOPES], F32)
    return jnp.stack((cs,) + _split3(cs), axis=1)


def _attention(q_hm, k_prev_bf, v_prev_hm, kp_new, kmax, k_new, v_hm, gm, lam_vecs, subln, *, pos0):
    tab = _alibi_table()
    if pos0:
        kp_past, kmax_past = _keyprep(k_prev_bf, gm)
        kmax = jnp.maximum(kmax, kmax_past)
        v_past = v_prev_hm
    else:
        kp_past, v_past = kp_new, v_hm
    att, bound = _attention_fast(tab, q_hm, kp_past, v_past, kp_new, v_hm, kmax, lam_vecs, subln, pos0=pos0)

    def online():
        rows = lambda x: jnp.transpose(x, (1, 2, 0, 3)).reshape(x.shape[1], x.shape[2], ATTN_WIDTH)
        k_bf = k_new.astype(BF16)
        k_all = jnp.concatenate([k_prev_bf, k_bf], axis=1) if pos0 else k_bf
        v_all = jnp.concatenate([rows(v_prev_hm), rows(v_hm)], axis=1) if pos0 else rows(v_hm)
        return _attention_online(tab, rows(q_hm), k_all, v_all, lam_vecs, subln, pos0=pos0)

    return lax.cond(jnp.max(bound) <= MAX_SAFE_BOUND, lambda: att, online)


def _outproj_kernel(x_ref, pool_ref, att_ref, wout_ref, g_ref, wq_ref, sk_ref, h_ref, hnt_ref, st_ref):
    mix = (jnp.dot(pool_ref[...], wout_ref[0:POOL_WIDTH, :], preferred_element_type=F32)
           + jnp.dot(att_ref[...], wout_ref[POOL_WIDTH:, :], preferred_element_type=F32))
    h = x_ref[...] + mix
    h_ref[...] = h
    hn = _rms_rows(h, g_ref[...])
    hnt_ref[...] = hn.T.astype(BF16)
    q = jnp.dot(hn.astype(BF16), wq_ref[...], preferred_element_type=F32)
    for g in range(2 * PEER_HEADS):
        qg = q[:, g * PEER_HALF:(g + 1) * PEER_HALF].astype(BF16)
        st_ref[g] = _dot_nt(sk_ref[g], qg)


def _outproj(x2d, pool_bf, att_bf, w_out_bf, norm_ffn, w_q_bf, sub_keys_bf):
    n = x2d.shape[0]
    tm = min(n, 256)
    row = lambda i: (i, 0)
    full2 = lambda i: (0, 0)
    return pl.pallas_call(
        _outproj_kernel,
        grid=(n // tm,),
        in_specs=[
            pl.BlockSpec((tm, D_MODEL), row),
            pl.BlockSpec((tm, POOL_WIDTH), row),
            pl.BlockSpec((tm, ATTN_WIDTH), row),
            pl.BlockSpec((D_MODEL, D_MODEL), full2),
            pl.BlockSpec((1, D_MODEL), full2),
            pl.BlockSpec((D_MODEL, 2 * PEER_HEADS * PEER_HALF), full2),
            pl.BlockSpec((2 * PEER_HEADS, PEER_KEYS, PEER_HALF), lambda i: (0, 0, 0)),
        ],
        out_specs=[
            pl.BlockSpec((tm, D_MODEL), row),
            pl.BlockSpec((D_MODEL, tm), lambda i: (0, i)),
            pl.BlockSpec((2 * PEER_HEADS, PEER_KEYS, tm), lambda i: (0, 0, i)),
        ],
        out_shape=[
            jax.ShapeDtypeStruct((n, D_MODEL), F32),
            jax.ShapeDtypeStruct((D_MODEL, n), BF16),
            jax.ShapeDtypeStruct((2 * PEER_HEADS, PEER_KEYS, n), F32),
        ],
        compiler_params=_params(("arbitrary",)),
        name="outproj_peer_query",
    )(x2d, pool_bf, att_bf, w_out_bf, norm_ffn, w_q_bf, sub_keys_bf)


def _take_max(cur, iota, exact):
    m = jnp.max(cur, axis=0, keepdims=True)
    if exact:
        idx = jnp.min(jnp.where(cur == m, iota, cur.shape[0]), axis=0, keepdims=True)
        return m, iota == idx
    return m, cur == m


def _top16_rows(s, iota_k, exact):
    tn = s.shape[1]
    row16 = lax.broadcasted_iota(jnp.int32, (PEER_TOPK, tn), 0)
    rank = jnp.full(s.shape, float(PEER_TOPK), F32)
    vals = jnp.zeros((PEER_TOPK, tn), F32)
    cur = s
    for i in range(PEER_TOPK):
        m, hit = _take_max(cur, iota_k, exact)
        rank = jnp.where(hit, float(i), rank)
        cur = jnp.where(hit, NEG_INF, cur)
        vals = jnp.where(row16 == i, m, vals)
    return vals, rank


_RANK_SUM = float(sum(range(PEER_TOPK)) + PEER_TOPK * (PEER_KEYS - PEER_TOPK))


def _topk_kernel(s_ref, cnt_ref, ea_ref, rank_ref, eb_ref, *, tn):
    surplus = _topk_heads(s_ref, cnt_ref, ea_ref, rank_ref, eb_ref, tn, exact=False)

    @pl.when(jnp.max(surplus) > 0.0)
    def _():
        _topk_heads(s_ref, cnt_ref, ea_ref, rank_ref, eb_ref, tn, exact=True)


def _topk_heads(s_ref, cnt_ref, ea_ref, rank_ref, eb_ref, tn, exact):
    iota_k = lax.broadcasted_iota(jnp.int32, (PEER_KEYS, tn), 0)
    iota_c = lax.broadcasted_iota(jnp.int32, (_CAND_ROWS, tn), 0)
    row8 = lax.broadcasted_iota(jnp.int32, (8, tn), 0)
    surplus = jnp.zeros((1, tn), F32)
    for h in range(PEER_HEADS):
        sa = s_ref[2 * h]
        sb = s_ref[2 * h + 1]
        va, rank_a = _top16_rows(sa, iota_k, exact)
        vb, rank_b = _top16_rows(sb, iota_k, exact)
        pieces = []
        for i0, ni, nj in _CAND_PIECES:
            if ni == 1:
                rows = 16 if nj > 8 else 8
                piece = va[i0:i0 + 1] + vb[0:rows]
                if nj < rows:
                    piece = jnp.where(row8 < nj, piece, NEG_INF)
            else:
                piece = va[i0:i0 + ni] + vb[0:1]
            pieces.append(piece)
        cand = jnp.concatenate(pieces, axis=0)
        cur = cand
        sel = jnp.zeros(cand.shape, F32)
        for _ in range(PEER_TOPK):
            _, hit = _take_max(cur, iota_c, exact)
            sel = jnp.where(hit, 1.0, sel)
            cur = jnp.where(hit, NEG_INF, cur)
        taken = (jnp.sum(rank_a, axis=0, keepdims=True) + jnp.sum(rank_b, axis=0, keepdims=True)
                 - jnp.sum(sel, axis=0, keepdims=True))
        surplus = jnp.maximum(surplus, jnp.abs(taken - (2.0 * _RANK_SUM - PEER_TOPK)))
        z = jnp.sum(jnp.where(sel > 0.0, jnp.exp(cand - cand[0:1]), 0.0), axis=0, keepdims=True)
        cnt = jnp.zeros(sa.shape, F32)
        off = 0
        for i0, ni, nj in _CAND_PIECES:
            rows = (16 if nj > 8 else 8) if ni == 1 else ni
            if ni == 1:
                c_i = jnp.sum(sel[off:off + rows], axis=0, keepdims=True)
                cnt = jnp.where(rank_a == float(i0), c_i, cnt)
            else:
                for r in range(ni):
                    cnt = jnp.where(rank_a == float(i0 + r), sel[off + r:off + r + 1], cnt)
            off += rows
        cnt_ref[h] = cnt
        ea_ref[h] = jnp.exp(sa - va[0:1])
        rank_ref[h] = rank_b.astype(BF16)
        eb_ref[h] = (jnp.exp(sb - vb[0:1]) / z).astype(BF16)
    return surplus


def _topk(scores_t):
    n = scores_t.shape[2]
    tn = min(n, 128)
    spec_in = pl.BlockSpec((2 * PEER_HEADS, PEER_KEYS, tn), lambda i: (0, 0, i))
    spec_out = pl.BlockSpec((PEER_HEADS, PEER_KEYS, tn), lambda i: (0, 0, i))
    shape_out = lambda dt: jax.ShapeDtypeStruct((PEER_HEADS, PEER_KEYS, n), dt)
    return pl.pallas_call(
        functools.partial(_topk_kernel, tn=tn),
        grid=(n // tn,),
        in_specs=[spec_in],
        out_specs=[spec_out] * 4,
        out_shape=[shape_out(F32), shape_out(F32), shape_out(BF16), shape_out(BF16)],
        compiler_params=_params(("arbitrary",)),
        name="peer_topk",
    )(scores_t)


def _peer_kernel(hnt_ref, h_ref, u_ref, vta_ref, vtb_ref, cnt_ref, ea_ref, rank_ref, eb_ref, y_ref,
                 acc_ref, acta_ref, actb_ref, *, te):
    step = pl.program_id(1)
    last = pl.num_programs(1) - 1

    @pl.when(step == 0)
    def _():
        acc_ref[...] = jnp.zeros_like(acc_ref)
        actb_ref[...] = jnp.zeros_like(actb_ref)

    hnt = hnt_ref[...]
    tn = hnt.shape[1]
    pack = 16
    tiles = PEER_KEYS // pack
    zero = jnp.zeros((tiles, pack, tn), BF16)

    def key_row(ref, h, a_glob):
        return jnp.broadcast_to(ref[h, pl.ds(a_glob, 1), :], (pack, tn)).astype(BF16)[None]

    def contribution(act_ref, vt_ref, block):
        coefs = []
        for r0 in range(0, te, PEER_KEYS):
            a_glob = block * (te // PEER_KEYS) + r0 // PEER_KEYS
            gate = None
            for h in range(PEER_HEADS):
                cnt = key_row(cnt_ref, h, a_glob)
                rank = rank_ref[h].reshape(tiles, pack, tn)
                term = jnp.where(rank < cnt, eb_ref[h].reshape(tiles, pack, tn), zero) * key_row(ea_ref, h, a_glob)
                gate = term if gate is None else gate + term
            x = act_ref[r0:r0 + PEER_KEYS, :]
            gelu = 0.5 * x * (1.0 + lax.erf(x * (2.0 ** -0.5)))
            coefs.append(gate.reshape(PEER_KEYS, tn) * gelu.astype(BF16))
        return jnp.dot(vt_ref[0], jnp.concatenate(coefs, axis=0), preferred_element_type=F32)

    acta_ref[...] = jnp.dot(u_ref[0:te, :], hnt, preferred_element_type=F32)
    part_b = contribution(actb_ref, vtb_ref, jnp.maximum(2 * step - 1, 0))
    actb_ref[...] = jnp.dot(u_ref[te:2 * te, :], hnt, preferred_element_type=F32)
    part_a = contribution(acta_ref, vta_ref, jnp.minimum(2 * step, 2 * last - 1))
    acc_ref[...] += jnp.where(step > 0, part_b, 0.0) + jnp.where(step < last, part_a, 0.0)

    @pl.when(step == last)
    def _():
        y_ref[...] = h_ref[...] + acc_ref[...].T


def _peer(hn_t, h2d, u_bf, vt_bf, cnt, ea, rank, eb):
    n = h2d.shape[0]
    tn = min(n, 512)
    te = PEER_BLOCK
    n_pairs = PEER_EXPERTS // (2 * te)
    tok = lambda i, s: (0, 0, i)
    fac = pl.BlockSpec((PEER_HEADS, PEER_KEYS, tn), tok)
    return pl.pallas_call(
        functools.partial(_peer_kernel, te=te),
        grid=(n // tn, n_pairs + 1),
        in_specs=[
            pl.BlockSpec((D_MODEL, tn), lambda i, s: (0, i)),
            pl.BlockSpec((tn, D_MODEL), lambda i, s: (i, 0)),
            pl.BlockSpec((2 * te, D_MODEL), lambda i, s: (jnp.minimum(s, n_pairs - 1), 0)),
            pl.BlockSpec((1, D_MODEL, te), lambda i, s: (jnp.minimum(2 * s, 2 * n_pairs - 1), 0, 0)),
            pl.BlockSpec((1, D_MODEL, te), lambda i, s: (jnp.maximum(2 * s - 1, 0), 0, 0)),
            fac, fac, fac, fac,
        ],
        out_specs=pl.BlockSpec((tn, D_MODEL), lambda i, s: (i, 0)),
        out_shape=jax.ShapeDtypeStruct((n, D_MODEL), F32),
        scratch_shapes=[pltpu.VMEM((D_MODEL, tn), F32), pltpu.VMEM((te, tn), F32), pltpu.VMEM((te, tn), F32)],
        compiler_params=_params(("arbitrary", "arbitrary")),
        name="peer_dense",
    )(hn_t, h2d, u_bf, vt_bf, vt_bf, cnt, ea, rank, eb)


def _stream(x, pool_prev, k_prev, v_prev, w, lam_vecs):
    b, t, _ = x.shape
    past = k_prev.shape[1]
    n = b * t
    x2d = x.reshape(n, D_MODEL)
    pprev = jnp.concatenate([jnp.zeros((b, POOL_HALO - POOL_TAIL, POOL_WIDTH), F32), pool_prev], axis=1)
    p, k_store, v, q_bf, kp_new, kmax, v_bf, pool_bf = _inproj(
        x2d, pprev, w["norm_mix"], w["w_in"], w["q_gain"], w["k_gain"], w["k_gain2_max"], w["group_ones"],
        w["w_pool"], w["pool_scale"], seq=t, pos0=past)
    att = _attention(q_bf.reshape(ATTN_HEADS, b, t, V_DIM),
                     k_prev.reshape(b, past, QK_WIDTH).astype(BF16),
                     jnp.transpose(v_prev, (2, 0, 1, 3)).astype(BF16),
                     kp_new, kmax, k_store.reshape(b, t, QK_WIDTH), v_bf.reshape(ATTN_HEADS, b, t, V_DIM),
                     w["group_ones"], lam_vecs, w["subln"], pos0=past)
    h2d, hn_t, scores_t = _outproj(x2d, pool_bf, att.reshape(n, ATTN_WIDTH), w["w_out"], w["norm_ffn"],
                                   w["w_peer_q"], w["sub_keys"])
    cnt, ea, rank, eb = _topk(scores_t)
    y = _peer(hn_t, h2d, w["peer_u"], w["peer_vt"], cnt, ea, rank, eb)
    tail = jnp.concatenate([pool_prev, p.reshape(b, t, POOL_WIDTH)], axis=1)[:, -POOL_TAIL:]
    return (y.reshape(b, t, D_MODEL), k_store.reshape(b, t, ATTN_HEADS, 2 * QK_DIM),
            v.reshape(b, t, ATTN_HEADS, V_DIM), tail)


def kernel(x_prompt, x_sample, cache_k, cache_v, state_pool, norm_mix, w_in, q_norm, k_norm, lambda_q1, lambda_k1, lambda_q2, lambda_k2, subln, w_pool, pool_scale, w_out, norm_ffn, w_peer_q, peer_sub_keys, peer_u, peer_v):
    depth = w_in.shape[0]
    assert depth == 1
    l = 0
    group = jnp.arange(QK_WIDTH) // QK_DIM
    w = {
        "norm_mix": norm_mix[l][None, :],
        "w_in": w_in[l].astype(BF16),
        "q_gain": jnp.tile(q_norm[l], QK_WIDTH // QK_DIM)[None, :],
        "k_gain": jnp.tile(k_norm[l], QK_WIDTH // QK_DIM)[None, :],
        "k_gain2_max": jnp.full((1, QK_WIDTH), jnp.max(jnp.square(k_norm[l])), F32),
        "group_ones": (group[:, None] == group[None, :]).astype(BF16),
        "w_pool": w_pool[l].astype(BF16),
        "pool_scale": pool_scale[l][None, :],
        "subln": subln[l][None, :],
        "w_out": w_out[l].astype(BF16),
        "norm_ffn": norm_ffn[l][None, :],
        "w_peer_q": w_peer_q[l].astype(BF16),
        "sub_keys": peer_sub_keys[l].reshape(2 * PEER_HEADS, PEER_KEYS, PEER_HALF).astype(BF16),
        "peer_u": peer_u[l].astype(BF16),
        "peer_vt": jnp.transpose(peer_v[l].reshape(PEER_EXPERTS // PEER_BLOCK, PEER_BLOCK, D_MODEL),
                                 (0, 2, 1)).astype(BF16),
    }
    lam_vecs = jnp.stack([lambda_q1[l], lambda_k1[l], lambda_q2[l], lambda_k2[l]]).astype(F32)
    bp = x_prompt.shape[0]
    zero_pool = jnp.zeros((bp, POOL_TAIL, POOL_WIDTH), F32)
    zero_k = jnp.zeros((bp, 0, ATTN_HEADS, 2 * QK_DIM), F32)
    zero_v = jnp.zeros((bp, 0, ATTN_HEADS, V_DIM), F32)
    yp, kp, vp, pp = _stream(x_prompt, zero_pool, zero_k, zero_v, w, lam_vecs)
    ys, kn, vn, pn = _stream(x_sample, state_pool[l], cache_k[l], cache_v[l], w, lam_vecs)
    return (yp, ys, kp[None], vp[None], pp[None], kn[None], vn[None], pn[None])
```

```python
import functools
import math

import jax
import jax.numpy as jnp
from jax import lax
from jax.experimental import pallas as pl
from jax.experimental.pallas import tpu as pltpu

F32 = jnp.float32
BF16 = jnp.bfloat16

D_MODEL = 1024
CHUNK = 64
POOL_WIDTH = 512
POOL_WINDOWS = (2, 4, 8, 16)
POOL_GROUP = 128
POOL_TAIL = 15
POOL_HALO = 16
ATTN_HEADS = 4
QK_DIM = 64
V_DIM = 128
QK_WIDTH = 512
ATTN_WIDTH = 512
IN_WIDTH = 2048
ALIBI_SLOPES = tuple(2.0 ** (-8.0 * (h + 1) / ATTN_HEADS) for h in range(ATTN_HEADS))
PEER_HEADS = 8
PEER_KEYS = 128
PEER_EXPERTS = PEER_KEYS * PEER_KEYS
PEER_HALF = 128
PEER_TOPK = 16
PEER_BLOCK = 1024
NORM_EPS = 1e-6
NEG_INF = -1e30
LAM_INIT = 0.8 - 0.6 * math.exp(-0.3 * 0)
LOG2E = 1.4426950408889634
KEY_BLOCK = 512
POS_PERIOD = 256
MAX_SAFE_BOUND = 60.0

VMEM_LIMIT_BYTES = 56 * 1024 * 1024

_CAND_PIECES = ((0, 1, 16), (1, 1, 8), (2, 1, 5), (3, 1, 4), (4, 1, 3), (5, 1, 2), (6, 1, 2), (7, 1, 2), (8, 8, 1))
_CAND_ROWS = 80


def _params(semantics):
    return pltpu.CompilerParams(dimension_semantics=semantics, vmem_limit_bytes=VMEM_LIMIT_BYTES)


def _rms_rows(x, gain):
    ms = jnp.mean(x * x, axis=-1, keepdims=True)
    return x * lax.rsqrt(ms + NORM_EPS) * gain


def _dot_nt(a, b):
    return lax.dot_general(a, b, (((1,), (1,)), ((), ())), preferred_element_type=F32)


_AUG_START = (QK_DIM, 0)


def _store_aug_keys(kp_ref, k, row0):
    tm = k.shape[0]
    lane = lax.broadcasted_iota(jnp.int32, (tm, V_DIM), 1)
    pos = ((lax.broadcasted_iota(jnp.int32, (tm, V_DIM), 0) + row0) & (POS_PERIOD - 1)).astype(F32)
    for h in range(ATTN_HEADS):
        kh = k[:, h * V_DIM:(h + 1) * V_DIM]
        for sub in range(2):
            a0 = _AUG_START[sub]
            aug = jnp.where((lane >= a0) & (lane < a0 + 3), 1.0,
                            jnp.where((lane >= a0 + 3) & (lane < a0 + 6), pos, 0.0))
            own = (lane < QK_DIM) if sub == 0 else (lane >= QK_DIM)
            kp_ref[0, h, :, sub * V_DIM:(sub + 1) * V_DIM] = jnp.where(own, kh, aug).astype(BF16)


def _running_max(ref, first, tile_max):
    @pl.when(first)
    def _():
        ref[0] = tile_max

    @pl.when(jnp.logical_not(first))
    def _():
        ref[0] = jnp.maximum(ref[0], tile_max)


def _inproj_kernel(x_ref, g_ref, win_ref, qg_ref, kg_ref, kg2_ref, gm_ref, pprev_ref, wpool_ref, pscale_ref,
                   p_ref, ks_ref, v_ref, qb_ref, kp_ref, kmax_ref, vb_ref, pool_ref, ext_ref,
                   *, tm, tiles_per_batch, pos0):
    t_in_batch = lax.rem(pl.program_id(0), tiles_per_batch)
    hn = _rms_rows(x_ref[...], g_ref[...]).astype(BF16)
    proj = jnp.dot(hn, win_ref[...], preferred_element_type=F32)
    p = proj[:, :POOL_WIDTH]
    q = proj[:, POOL_WIDTH:POOL_WIDTH + QK_WIDTH]
    k = proj[:, POOL_WIDTH + QK_WIDTH:POOL_WIDTH + 2 * QK_WIDTH]
    v = proj[:, POOL_WIDTH + 2 * QK_WIDTH:]

    def group_rms(z, gain):
        sq = z * z
        hi = sq.astype(BF16)
        lo = (sq - hi.astype(F32)).astype(BF16)
        ss = (jnp.dot(hi, gm_ref[...], preferred_element_type=F32)
              + jnp.dot(lo, gm_ref[...], preferred_element_type=F32))
        inv = lax.rsqrt(ss * (1.0 / QK_DIM) + NORM_EPS)
        return z * inv * gain, ss * inv * inv

    qn, _ = group_rms(q, qg_ref[...])
    kn, k_unit = group_rms(k, kg_ref[...])
    p_ref[...] = p
    for h in range(ATTN_HEADS):
        ks_ref[pl.ds(h, tm, stride=ATTN_HEADS), :] = kn[:, h * V_DIM:(h + 1) * V_DIM]
        v_ref[pl.ds(h, tm, stride=ATTN_HEADS), :] = v[:, h * V_DIM:(h + 1) * V_DIM]
    qb = (qn * (QK_DIM ** -0.5 * LOG2E)).astype(BF16)
    vb = v.astype(BF16)
    for h in range(ATTN_HEADS):
        qb_ref[h] = qb[:, h * V_DIM:(h + 1) * V_DIM]
        vb_ref[h] = vb[:, h * V_DIM:(h + 1) * V_DIM]
    _store_aug_keys(kp_ref, kn, pos0 + t_in_batch * tm)
    _running_max(kmax_ref, t_in_batch == 0, jnp.max(k_unit * kg2_ref[...], axis=0, keepdims=True))

    @pl.when(t_in_batch == 0)
    def _():
        ext_ref[0:POOL_HALO, :] = pprev_ref[0]

    ext_ref[POOL_HALO:POOL_HALO + tm, :] = p
    pos = lax.broadcasted_iota(jnp.int32, (tm, POOL_GROUP), 0) + (pos0 + t_in_batch * tm)
    mixed = []
    for g, w in enumerate(POOL_WINDOWS):
        lanes = slice(g * POOL_GROUP, (g + 1) * POOL_GROUP)
        win = ext_ref[POOL_HALO:POOL_HALO + tm, lanes]
        for j in range(1, w):
            win = win + ext_ref[POOL_HALO - j:POOL_HALO - j + tm, lanes]
        cnt = jnp.minimum(w, pos + 1).astype(F32)
        pooled = win / cnt - p[:, lanes]
        mixed.append(jnp.dot(pooled.astype(BF16), wpool_ref[g], preferred_element_type=F32))
    pool_ref[...] = (jnp.concatenate(mixed, axis=1) * pscale_ref[...]).astype(BF16)
    ext_ref[0:POOL_HALO, :] = ext_ref[tm:tm + POOL_HALO, :]


def _inproj(x2d, pprev, norm_mix, w_in_bf, qg, kg, kg2, gm, w_pool_bf, pool_scale, *, seq, pos0):
    n = x2d.shape[0]
    b = n // seq
    tm = min(seq, 512)
    tiles_per_batch = seq // tm
    row = lambda i: (i, 0)
    full2 = lambda i: (0, 0)
    rows_spec = pl.BlockSpec((tm, POOL_WIDTH), row)
    out_w = lambda dt: jax.ShapeDtypeStruct((n, POOL_WIDTH), dt)
    out_h = jax.ShapeDtypeStruct((ATTN_HEADS, n, V_DIM), BF16)
    by_head = pl.BlockSpec((ATTN_HEADS, tm, V_DIM), lambda i: (0, i, 0))
    out_cache = jax.ShapeDtypeStruct((n * ATTN_HEADS, V_DIM), F32)
    cache_spec = pl.BlockSpec((tm * ATTN_HEADS, V_DIM), row)
    return pl.pallas_call(
        functools.partial(_inproj_kernel, tm=tm, tiles_per_batch=tiles_per_batch, pos0=pos0),
        grid=(n // tm,),
        in_specs=[
            pl.BlockSpec((tm, D_MODEL), row),
            pl.BlockSpec((1, D_MODEL), full2),
            pl.BlockSpec((D_MODEL, IN_WIDTH), full2),
            pl.BlockSpec((1, QK_WIDTH), full2),
            pl.BlockSpec((1, QK_WIDTH), full2),
            pl.BlockSpec((1, QK_WIDTH), full2),
            pl.BlockSpec((QK_WIDTH, QK_WIDTH), full2),
            pl.BlockSpec((1, POOL_HALO, POOL_WIDTH), lambda i: (i // tiles_per_batch, 0, 0)),
            pl.BlockSpec((len(POOL_WINDOWS), POOL_GROUP, POOL_GROUP), lambda i: (0, 0, 0)),
            pl.BlockSpec((1, POOL_WIDTH), full2),
        ],
        out_specs=[
            rows_spec, cache_spec, cache_spec, by_head,
            pl.BlockSpec((1, ATTN_HEADS, tm, 2 * V_DIM),
                         lambda i: (i // tiles_per_batch, 0, lax.rem(i, tiles_per_batch), 0)),
            pl.BlockSpec((1, 1, QK_WIDTH), lambda i: (i // tiles_per_batch, 0, 0)),
            by_head, rows_spec,
        ],
        out_shape=[out_w(F32), out_cache, out_cache, out_h,
                   jax.ShapeDtypeStruct((b, ATTN_HEADS, seq, 2 * V_DIM), BF16),
                   jax.ShapeDtypeStruct((b, 1, QK_WIDTH), F32), out_h, out_w(BF16)],
        scratch_shapes=[pltpu.VMEM((tm + POOL_HALO, POOL_WIDTH), F32)],
        compiler_params=_params(("arbitrary",)),
        name="inproj_pool",
    )(x2d, norm_mix, w_in_bf, qg, kg, kg2, gm, pprev, w_pool_bf, pool_scale)


def _keyprep_kernel(k_ref, gm_ref, kp_ref, kmax_ref, *, tm):
    t = pl.program_id(1)
    k = k_ref[0].astype(F32)
    ss = jnp.dot((k * k).astype(BF16), gm_ref[...], preferred_element_type=F32)
    _running_max(kmax_ref, t == 0, jnp.max(ss, axis=0, keepdims=True))
    _store_aug_keys(kp_ref, k, t * tm)


def _keyprep(k3d, gm):
    b, s, _ = k3d.shape
    tm = min(s, 512)
    return pl.pallas_call(
        functools.partial(_keyprep_kernel, tm=tm),
        grid=(b, s // tm),
        in_specs=[
            pl.BlockSpec((1, tm, QK_WIDTH), lambda bi, ti: (bi, ti, 0)),
            pl.BlockSpec((QK_WIDTH, QK_WIDTH), lambda bi, ti: (0, 0)),
        ],
        out_specs=[
            pl.BlockSpec((1, ATTN_HEADS, tm, 2 * V_DIM), lambda bi, ti: (bi, 0, ti, 0)),
            pl.BlockSpec((1, 1, QK_WIDTH), lambda bi, ti: (bi, 0, 0)),
        ],
        out_shape=[
            jax.ShapeDtypeStruct((b, ATTN_HEADS, s, 2 * V_DIM), BF16),
            jax.ShapeDtypeStruct((b, 1, QK_WIDTH), F32),
        ],
        compiler_params=_params(("arbitrary", "arbitrary")),
        name="attn_keyprep",
    )(k3d, gm)


def _split3(a):
    hi = a.astype(BF16).astype(F32)
    r = a - hi
    mid = r.astype(BF16).astype(F32)
    return hi, mid, r - mid


def _diff_finish(acc, l, lam_ref, subln_ref, tq):
    lam_v = lam_ref[...]
    lam = (jnp.exp(jnp.sum(lam_v[0:1] * lam_v[1:2], axis=1, keepdims=True))
           - jnp.exp(jnp.sum(lam_v[2:3] * lam_v[3:4], axis=1, keepdims=True)) + LAM_INIT)
    o = acc / l
    o = o[:tq] - lam * o[tq:]
    return (_rms_rows(o, subln_ref[...]) * (1.0 - LAM_INIT)).astype(BF16)


def _attn_fast_kernel(tab_ref, q_ref, kpast_ref, vpast_ref, kdiag_ref, vdiag_ref, kmax_ref, lam_ref, subln_ref,
                      o_ref, bound_ref, acc_ref, l_ref, bdiag_ref, s_ref, *, tq, tk, pos0):
    h = pl.program_id(1)
    qi = pl.program_id(2)
    slope = tab_ref[h, 0]

    @pl.when(qi == 0)
    def _():
        r = lax.rem(lax.broadcasted_iota(jnp.int32, (2 * tq, tq), 0), tq)
        c = lax.broadcasted_iota(jnp.int32, (2 * tq, tq), 1)
        allowed = (c // CHUNK) <= (r // CHUNK)
        bdiag_ref[...] = jnp.where(allowed, -slope * jnp.abs(r - c).astype(F32), NEG_INF)

    q = q_ref[0, 0].astype(F32)
    lane = lax.broadcasted_iota(jnp.int32, q.shape, 1)
    first = lane < QK_DIM
    qsq = q * q
    km = kmax_ref[0]
    bounds = (jnp.sqrt(jnp.sum(jnp.where(first, qsq, 0.0), axis=1, keepdims=True) * km[:, 0:1]),
              jnp.sqrt(jnp.sum(jnp.where(first, 0.0, qsq), axis=1, keepdims=True) * km[:, QK_DIM:QK_DIM + 1]))
    bound_ref[0, 0, 0] = jnp.broadcast_to(
        jnp.maximum(jnp.max(bounds[0], axis=0, keepdims=True), jnp.max(bounds[1], axis=0, keepdims=True)),
        (8, V_DIM))
    row = lax.broadcasted_iota(jnp.int32, (tq, 1), 0).astype(F32)

    def augmented(sub, with_alibi):
        a0 = _AUG_START[sub]
        offset = -(slope * row + bounds[sub]) if with_alibi else -bounds[sub]
        pieces = _split3(offset) + ((tab_ref[h, 1], tab_ref[h, 2], tab_ref[h, 3]) if with_alibi else ())
        aug = jnp.zeros(q.shape, F32)
        for i, piece in enumerate(pieces):
            aug = jnp.where(lane == a0 + i, piece, aug)
        own = first if sub == 0 else jnp.logical_not(first)
        return jnp.where(own, q, aug).astype(BF16)

    acc_ref[...] = jnp.zeros_like(acc_ref)
    l_ref[...] = jnp.zeros_like(l_ref)

    def consume(s, vblk):
        p = jnp.exp2(s)
        width = p.shape[1]
        if width >= V_DIM:
            part = p[:, 0:V_DIM]
            for c0 in range(V_DIM, width, V_DIM):
                part = part + p[:, c0:c0 + V_DIM]
            l_ref[...] += part
        else:
            l_ref[:, 0:width] += p
        acc_ref[...] += jnp.dot(p.astype(BF16), vblk, preferred_element_type=F32)

    base = pos0 + qi * tq
    q_past = (augmented(0, True), augmented(1, True))

    n_past = base // tk
    n_wide = n_past // 2
    col = lax.broadcasted_iota(jnp.int32, (1, 2 * tk), 1)
    col_offset = slope * (col - (col & (POS_PERIOD - 1))).astype(F32)

    def past_exponents(start, width):
        start = start if isinstance(start, int) else pl.multiple_of(start, tk)
        s = jnp.concatenate([_dot_nt(q_past[0], kpast_ref[0, 0, pl.ds(start, width), 0:V_DIM]),
                             _dot_nt(q_past[1], kpast_ref[0, 0, pl.ds(start, width), V_DIM:2 * V_DIM])], axis=0)
        return s + (col_offset[:, 0:width] - slope * (base - start).astype(F32))

    def past_values(start, width):
        start = start if isinstance(start, int) else pl.multiple_of(start, tk)
        return vpast_ref[0, 0, pl.ds(start, width), :]

    @pl.when(n_wide > 0)
    def _():
        s_ref[...] = past_exponents(0, 2 * tk)

    def past_block(j, _):
        s_next = past_exponents((j + 1) * (2 * tk), 2 * tk)
        consume(s_ref[...], past_values(j * (2 * tk), 2 * tk))
        s_ref[...] = s_next
        return 0

    lax.fori_loop(0, n_wide - 1, past_block, 0)

    @pl.when(n_wide > 0)
    def _():
        consume(s_ref[...], past_values((n_wide - 1) * (2 * tk), 2 * tk))

    @pl.when(n_past > 2 * n_wide)
    def _():
        start = (n_past - 1) * tk
        consume(past_exponents(start, tk), past_values(start, tk))

    s = jnp.concatenate([_dot_nt(augmented(0, False), kdiag_ref[0, 0, :, 0:V_DIM]),
                         _dot_nt(augmented(1, False), kdiag_ref[0, 0, :, V_DIM:2 * V_DIM])], axis=0)
    consume(s + bdiag_ref[...], vdiag_ref[0, 0])
    l = jnp.sum(l_ref[...], axis=1, keepdims=True)
    o_ref[0] = _diff_finish(acc_ref[...], l, lam_ref, subln_ref, tq)


def _attention_fast(tab, q_hm, kp_past, v_past, kp_new, v_new, kmax, lam_vecs, subln, *, pos0):
    _, b, t, _ = q_hm.shape
    s_past = kp_past.shape[2]
    tq = min(t, KEY_BLOCK)
    tk = KEY_BLOCK
    nq = t // tq
    assert pos0 % tk == 0 and t % tq == 0 and (tq == tk or nq == 1) and pos0 % CHUNK == 0 and tq % 8 == 0
    assert s_past >= pos0 + t - tq
    const2 = lambda bi, hi, qi: (0, 0)
    return pl.pallas_call(
        functools.partial(_attn_fast_kernel, tq=tq, tk=tk, pos0=pos0),
        grid=(b, ATTN_HEADS, nq),
        in_specs=[
            pl.BlockSpec(memory_space=pltpu.SMEM),
            pl.BlockSpec((1, 1, tq, V_DIM), lambda bi, hi, qi: (hi, bi, qi, 0)),
            pl.BlockSpec((1, 1, s_past, 2 * V_DIM), lambda bi, hi, qi: (bi, hi, 0, 0)),
            pl.BlockSpec((1, 1, s_past, V_DIM), lambda bi, hi, qi: (hi, bi, 0, 0)),
            pl.BlockSpec((1, 1, tq, 2 * V_DIM), lambda bi, hi, qi: (bi, hi, qi, 0)),
            pl.BlockSpec((1, 1, tq, V_DIM), lambda bi, hi, qi: (hi, bi, qi, 0)),
            pl.BlockSpec((1, 1, V_DIM), lambda bi, hi, qi: (bi, 0, hi)),
            pl.BlockSpec((4, QK_DIM), const2),
            pl.BlockSpec((1, V_DIM), const2),
        ],
        out_specs=[
            pl.BlockSpec((1, tq, V_DIM), lambda bi, hi, qi: (bi, qi, hi)),
            pl.BlockSpec((1, 1, 1, 8, V_DIM), lambda bi, hi, qi: (bi, hi, qi, 0, 0)),
        ],
        out_shape=[
            jax.ShapeDtypeStruct((b, t, ATTN_WIDTH), BF16),
            jax.ShapeDtypeStruct((b, ATTN_HEADS, nq, 8, V_DIM), F32),
        ],
        scratch_shapes=[pltpu.VMEM((2 * tq, V_DIM), F32), pltpu.VMEM((2 * tq, V_DIM), F32),
                        pltpu.VMEM((2 * tq, tq), F32), pltpu.VMEM((2 * tq, 2 * tk), F32)],
        compiler_params=_params(("arbitrary", "arbitrary", "arbitrary")),
        name="diff_attention",
    )(tab, q_hm, kp_past, v_past, kp_new, v_new, kmax, lam_vecs, subln)


def _attn_online_kernel(tab_ref, q_ref, k_ref, v_ref, lam_ref, subln_ref, o_ref, boff_ref, bdiag_ref,
                        *, tq, tk, pos0):
    h = pl.program_id(1)
    qi = pl.program_id(2)
    slope = tab_ref[h, 0]

    @pl.when(qi == 0)
    def _():
        r = lax.rem(lax.broadcasted_iota(jnp.int32, (2 * tq, tk), 0), tq)
        c = lax.broadcasted_iota(jnp.int32, (2 * tq, tk), 1)
        boff_ref[...] = -slope * (r - c).astype(F32)
        r = lax.rem(lax.broadcasted_iota(jnp.int32, (2 * tq, tq), 0), tq)
        c = lax.broadcasted_iota(jnp.int32, (2 * tq, tq), 1)
        allowed = (c // CHUNK) <= (r // CHUNK)
        bdiag_ref[...] = jnp.where(allowed, -slope * jnp.abs(r - c).astype(F32), NEG_INF)

    q = q_ref[0]
    lane = lax.broadcasted_iota(jnp.int32, q.shape, 1)
    zero = jnp.zeros_like(q)
    qs = jnp.concatenate([jnp.where(lane < QK_DIM, q, zero), jnp.where(lane >= QK_DIM, q, zero)], axis=0)
    base = pos0 + qi * tq

    def update(s, vblk, carry):
        m, l, acc = carry
        m_new = jnp.maximum(m, jnp.max(s, axis=1, keepdims=True))
        alpha = jnp.exp2(m - m_new)
        p = jnp.exp2(s - m_new)
        l = alpha * l + jnp.sum(p, axis=1, keepdims=True)
        acc = alpha * acc + jnp.dot(p.astype(BF16), vblk, preferred_element_type=F32)
        return m_new, l, acc

    def past_block(kb, carry):
        start = pl.multiple_of(kb * tk, tk)
        s = _dot_nt(qs, k_ref[0, pl.ds(start, tk), :])
        s = s + boff_ref[...] - slope * (base - kb * tk).astype(F32)
        return update(s, v_ref[0, pl.ds(start, tk), :], carry)

    carry = (jnp.full((2 * tq, 1), NEG_INF, F32), jnp.zeros((2 * tq, 1), F32), jnp.zeros((2 * tq, V_DIM), F32))
    carry = lax.fori_loop(0, base // tk, past_block, carry)
    dstart = pl.multiple_of(base, tq)
    s = _dot_nt(qs, k_ref[0, pl.ds(dstart, tq), :]) + bdiag_ref[...]
    _, l, acc = update(s, v_ref[0, pl.ds(dstart, tq), :], carry)
    o_ref[0] = _diff_finish(acc, l, lam_ref, subln_ref, tq)


def _attention_online(tab, q_bf, k_all, v_all, lam_vecs, subln, *, pos0):
    b, t, _ = q_bf.shape
    s = k_all.shape[1]
    tq = min(t, KEY_BLOCK)
    tk = KEY_BLOCK
    assert pos0 % tk == 0 and t % tq == 0 and (tq == tk or t == tq) and pos0 % CHUNK == 0 and tq % 8 == 0
    assert s == pos0 + t
    return pl.pallas_call(
        functools.partial(_attn_online_kernel, tq=tq, tk=tk, pos0=pos0),
        grid=(b, ATTN_HEADS, t // tq),
        in_specs=[
            pl.BlockSpec(memory_space=pltpu.SMEM),
            pl.BlockSpec((1, tq, V_DIM), lambda bi, hi, qi: (bi, qi, hi)),
            pl.BlockSpec((1, s, V_DIM), lambda bi, hi, qi: (bi, 0, hi)),
            pl.BlockSpec((1, s, V_DIM), lambda bi, hi, qi: (bi, 0, hi)),
            pl.BlockSpec((4, QK_DIM), lambda bi, hi, qi: (0, 0)),
            pl.BlockSpec((1, V_DIM), lambda bi, hi, qi: (0, 0)),
        ],
        out_specs=pl.BlockSpec((1, tq, V_DIM), lambda bi, hi, qi: (bi, qi, hi)),
        out_shape=jax.ShapeDtypeStruct((b, t, ATTN_WIDTH), BF16),
        scratch_shapes=[pltpu.VMEM((2 * tq, tk), F32), pltpu.VMEM((2 * tq, tq), F32)],
        compiler_params=_params(("arbitrary", "arbitrary", "arbitrary")),
        name="diff_attention_online",
    )(tab, q_bf, k_all, v_all, lam_vecs, subln)


def _alibi_table():
    cs = jnp.asarray([LOG2E * s for s in ALIBI_SLOPES], F32)
    return jnp.stack((cs,) + _split3(cs), axis=1)


def _attention(q_hm, k_prev_bf, v_prev_hm, kp_new, kmax, k_new, v_hm, gm, lam_vecs, subln, *, pos0):
    tab = _alibi_table()
    if pos0:
        kp_past, kmax_past = _keyprep(k_prev_bf, gm)
        kmax = jnp.maximum(kmax, kmax_past)
        v_past = v_prev_hm
    else:
        kp_past, v_past = kp_new, v_hm
    att, bound = _attention_fast(tab, q_hm, kp_past, v_past, kp_new, v_hm, kmax, lam_vecs, subln, pos0=pos0)

    def online():
        rows = lambda x: jnp.transpose(x, (1, 2, 0, 3)).reshape(x.shape[1], x.shape[2], ATTN_WIDTH)
        k_bf = k_new.astype(BF16)
        k_all = jnp.concatenate([k_prev_bf, k_bf], axis=1) if pos0 else k_bf
        v_all = jnp.concatenate([rows(v_prev_hm), rows(v_hm)], axis=1) if pos0 else rows(v_hm)
        return _attention_online(tab, rows(q_hm), k_all, v_all, lam_vecs, subln, pos0=pos0)

    return lax.cond(jnp.max(bound) <= MAX_SAFE_BOUND, lambda: att, online)


def _outproj_kernel(x_ref, pool_ref, att_ref, wout_ref, g_ref, wq_ref, sk_ref, h_ref, hnt_ref, st_ref):
    mix = (jnp.dot(pool_ref[...], wout_ref[0:POOL_WIDTH, :], preferred_element_type=F32)
           + jnp.dot(att_ref[...], wout_ref[POOL_WIDTH:, :], preferred_element_type=F32))
    h = x_ref[...] + mix
    h_ref[...] = h
    hn = _rms_rows(h, g_ref[...])
    hnt_ref[...] = hn.T.astype(BF16)
    q = jnp.dot(hn.astype(BF16), wq_ref[...], preferred_element_type=F32)
    for g in range(2 * PEER_HEADS):
        qg = q[:, g * PEER_HALF:(g + 1) * PEER_HALF].astype(BF16)
        st_ref[g] = _dot_nt(sk_ref[g], qg)


def _outproj(x2d, pool_bf, att_bf, w_out_bf, norm_ffn, w_q_bf, sub_keys_bf):
    n = x2d.shape[0]
    tm = min(n, 256)
    row = lambda i: (i, 0)
    full2 = lambda i: (0, 0)
    return pl.pallas_call(
        _outproj_kernel,
        grid=(n // tm,),
        in_specs=[
            pl.BlockSpec((tm, D_MODEL), row),
            pl.BlockSpec((tm, POOL_WIDTH), row),
            pl.BlockSpec((tm, ATTN_WIDTH), row),
            pl.BlockSpec((D_MODEL, D_MODEL), full2),
            pl.BlockSpec((1, D_MODEL), full2),
            pl.BlockSpec((D_MODEL, 2 * PEER_HEADS * PEER_HALF), full2),
            pl.BlockSpec((2 * PEER_HEADS, PEER_KEYS, PEER_HALF), lambda i: (0, 0, 0)),
        ],
        out_specs=[
            pl.BlockSpec((tm, D_MODEL), row),
            pl.BlockSpec((D_MODEL, tm), lambda i: (0, i)),
            pl.BlockSpec((2 * PEER_HEADS, PEER_KEYS, tm), lambda i: (0, 0, i)),
        ],
        out_shape=[
            jax.ShapeDtypeStruct((n, D_MODEL), F32),
            jax.ShapeDtypeStruct((D_MODEL, n), BF16),
            jax.ShapeDtypeStruct((2 * PEER_HEADS, PEER_KEYS, n), F32),
        ],
        compiler_params=_params(("arbitrary",)),
        name="outproj_peer_query",
    )(x2d, pool_bf, att_bf, w_out_bf, norm_ffn, w_q_bf, sub_keys_bf)


def _take_max(cur, iota, exact):
    m = jnp.max(cur, axis=0, keepdims=True)
    if exact:
        idx = jnp.min(jnp.where(cur == m, iota, cur.shape[0]), axis=0, keepdims=True)
        return m, iota == idx
    return m, cur == m


def _top16_rows(s, iota_k, exact):
    tn = s.shape[1]
    row16 = lax.broadcasted_iota(jnp.int32, (PEER_TOPK, tn), 0)
    rank = jnp.full(s.shape, float(PEER_TOPK), F32)
    vals = jnp.zeros((PEER_TOPK, tn), F32)
    cur = s
    for i in range(PEER_TOPK):
        m, hit = _take_max(cur, iota_k, exact)
        rank = jnp.where(hit, float(i), rank)
        cur = jnp.where(hit, NEG_INF, cur)
        vals = jnp.where(row16 == i, m, vals)
    return vals, rank


_RANK_SUM = float(sum(range(PEER_TOPK)) + PEER_TOPK * (PEER_KEYS - PEER_TOPK))


def _topk_kernel(s_ref, cnt_ref, ea_ref, rank_ref, eb_ref, *, tn):
    surplus = _topk_heads(s_ref, cnt_ref, ea_ref, rank_ref, eb_ref, tn, exact=False)

    @pl.when(jnp.max(surplus) > 0.0)
    def _():
        _topk_heads(s_ref, cnt_ref, ea_ref, rank_ref, eb_ref, tn, exact=True)


def _topk_heads(s_ref, cnt_ref, ea_ref, rank_ref, eb_ref, tn, exact):
    iota_k = lax.broadcasted_iota(jnp.int32, (PEER_KEYS, tn), 0)
    iota_c = lax.broadcasted_iota(jnp.int32, (_CAND_ROWS, tn), 0)
    row8 = lax.broadcasted_iota(jnp.int32, (8, tn), 0)
    surplus = jnp.zeros((1, tn), F32)
    for h in range(PEER_HEADS):
        sa = s_ref[2 * h]
        sb = s_ref[2 * h + 1]
        va, rank_a = _top16_rows(sa, iota_k, exact)
        vb, rank_b = _top16_rows(sb, iota_k, exact)
        pieces = []
        for i0, ni, nj in _CAND_PIECES:
            if ni == 1:
                rows = 16 if nj > 8 else 8
                piece = va[i0:i0 + 1] + vb[0:rows]
                if nj < rows:
                    piece = jnp.where(row8 < nj, piece, NEG_INF)
            else:
                piece = va[i0:i0 + ni] + vb[0:1]
            pieces.append(piece)
        cand = jnp.concatenate(pieces, axis=0)
        cur = cand
        sel = jnp.zeros(cand.shape, F32)
        for _ in range(PEER_TOPK):
            _, hit = _take_max(cur, iota_c, exact)
            sel = jnp.where(hit, 1.0, sel)
            cur = jnp.where(hit, NEG_INF, cur)
        taken = (jnp.sum(rank_a, axis=0, keepdims=True) + jnp.sum(rank_b, axis=0, keepdims=True)
                 - jnp.sum(sel, axis=0, keepdims=True))
        surplus = jnp.maximum(surplus, jnp.abs(taken - (2.0 * _RANK_SUM - PEER_TOPK)))
        z = jnp.sum(jnp.where(sel > 0.0, jnp.exp(cand - cand[0:1]), 0.0), axis=0, keepdims=True)
        cnt = jnp.zeros(sa.shape, F32)
        off = 0
        for i0, ni, nj in _CAND_PIECES:
            rows = (16 if nj > 8 else 8) if ni == 1 else ni
            if ni == 1:
                c_i = jnp.sum(sel[off:off + rows], axis=0, keepdims=True)
                cnt = jnp.where(rank_a == float(i0), c_i, cnt)
            else:
                for r in range(ni):
                    cnt = jnp.where(rank_a == float(i0 + r), sel[off + r:off + r + 1], cnt)
            off += rows
        cnt_ref[h] = cnt
        ea_ref[h] = jnp.exp(sa - va[0:1])
        rank_ref[h] = rank_b.astype(BF16)
        eb_ref[h] = (jnp.exp(sb - vb[0:1]) / z).astype(BF16)
    return surplus


def _topk(scores_t):
    n = scores_t.shape[2]
    tn = min(n, 128)
    spec_in = pl.BlockSpec((2 * PEER_HEADS, PEER_KEYS, tn), lambda i: (0, 0, i))
    spec_out = pl.BlockSpec((PEER_HEADS, PEER_KEYS, tn), lambda i: (0, 0, i))
    shape_out = lambda dt: jax.ShapeDtypeStruct((PEER_HEADS, PEER_KEYS, n), dt)
    return pl.pallas_call(
        functools.partial(_topk_kernel, tn=tn),
        grid=(n // tn,),
        in_specs=[spec_in],
        out_specs=[spec_out] * 4,
        out_shape=[shape_out(F32), shape_out(F32), shape_out(BF16), shape_out(BF16)],
        compiler_params=_params(("arbitrary",)),
        name="peer_topk",
    )(scores_t)


def _peer_kernel(hnt_ref, h_ref, u_ref, vta_ref, vtb_ref, cnt_ref, ea_ref, rank_ref, eb_ref, y_ref,
                 acc_ref, acta_ref, actb_ref, *, te):
    step = pl.program_id(1)
    last = pl.num_programs(1) - 1
    hnt = hnt_ref[...]
    tn = hnt.shape[1]
    pack = 16
    tiles = PEER_KEYS // pack
    zero = jnp.zeros((tiles, pack, tn), BF16)

    def key_row(ref, h, a_glob):
        return jnp.broadcast_to(ref[h, pl.ds(a_glob, 1), :], (pack, tn)).astype(BF16)[None]

    def contribution(act_ref, vt_ref, block):
        coefs = []
        for r0 in range(0, te, PEER_KEYS):
            a_glob = block * (te // PEER_KEYS) + r0 // PEER_KEYS
            gate = None
            for h in range(PEER_HEADS):
                cnt = key_row(cnt_ref, h, a_glob)
                rank = rank_ref[h].reshape(tiles, pack, tn)
                term = jnp.where(rank < cnt, eb_ref[h].reshape(tiles, pack, tn), zero) * key_row(ea_ref, h, a_glob)
                gate = term if gate is None else gate + term
            x = act_ref[r0:r0 + PEER_KEYS, :]
            gelu = 0.5 * x * (1.0 + lax.erf(x * (2.0 ** -0.5)))
            coefs.append(gate.reshape(PEER_KEYS, tn) * gelu.astype(BF16))
        return jnp.dot(vt_ref[0], jnp.concatenate(coefs, axis=0), preferred_element_type=F32)

    def produce(act_ref, half):
        act_ref[...] = jnp.dot(u_ref[half * te:(half + 1) * te, :], hnt, preferred_element_type=F32)

    @pl.when(step == 0)
    def _():
        produce(acta_ref, 0)
        produce(actb_ref, 1)
        acc_ref[...] = contribution(acta_ref, vta_ref, 0)

    @pl.when(jnp.logical_and(step > 0, step < last))
    def _():
        produce(acta_ref, 0)
        part_b = contribution(actb_ref, vtb_ref, 2 * step - 1)
        produce(actb_ref, 1)
        acc_ref[...] += part_b + contribution(acta_ref, vta_ref, 2 * step)

    @pl.when(step == last)
    def _():
        total = acc_ref[...] + contribution(actb_ref, vtb_ref, 2 * step - 1)
        y_ref[...] = h_ref[...] + total.T


def _peer(hn_t, h2d, u_bf, vt_bf, cnt, ea, rank, eb):
    n = h2d.shape[0]
    tn = min(n, 512)
    te = PEER_BLOCK
    n_pairs = PEER_EXPERTS // (2 * te)
    tok = lambda i, s: (0, 0, i)
    fac = pl.BlockSpec((PEER_HEADS, PEER_KEYS, tn), tok)
    return pl.pallas_call(
        functools.partial(_peer_kernel, te=te),
        grid=(n // tn, n_pairs + 1),
        in_specs=[
            pl.BlockSpec((D_MODEL, tn), lambda i, s: (0, i)),
            pl.BlockSpec((tn, D_MODEL), lambda i, s: (i, 0)),
            pl.BlockSpec((2 * te, D_MODEL), lambda i, s: (jnp.minimum(s, n_pairs - 1), 0)),
            pl.BlockSpec((1, D_MODEL, te), lambda i, s: (jnp.minimum(2 * s, 2 * n_pairs - 1), 0, 0)),
            pl.BlockSpec((1, D_MODEL, te), lambda i, s: (jnp.maximum(2 * s - 1, 0), 0, 0)),
            fac, fac, fac, fac,
        ],
        out_specs=pl.BlockSpec((tn, D_MODEL), lambda i, s: (i, 0)),
        out_shape=jax.ShapeDtypeStruct((n, D_MODEL), F32),
        scratch_shapes=[pltpu.VMEM((D_MODEL, tn), F32), pltpu.VMEM((te, tn), F32), pltpu.VMEM((te, tn), F32)],
        compiler_params=_params(("arbitrary", "arbitrary")),
        name="peer_dense",
    )(hn_t, h2d, u_bf, vt_bf, vt_bf, cnt, ea, rank, eb)


def _stream(x, pool_prev, k_prev, v_prev, w, lam_vecs):
    b, t, _ = x.shape
    past = k_prev.shape[1]
    n = b * t
    x2d = x.reshape(n, D_MODEL)
    pprev = jnp.concatenate([jnp.zeros((b, POOL_HALO - POOL_TAIL, POOL_WIDTH), F32), pool_prev], axis=1)
    p, k_store, v, q_bf, kp_new, kmax, v_bf, pool_bf = _inproj(
        x2d, pprev, w["norm_mix"], w["w_in"], w["q_gain"], w["k_gain"], w["k_gain2_max"], w["group_ones"],
        w["w_pool"], w["pool_scale"], seq=t, pos0=past)
    att = _attention(q_bf.reshape(ATTN_HEADS, b, t, V_DIM),
                     k_prev.reshape(b, past, QK_WIDTH).astype(BF16),
                     jnp.transpose(v_prev, (2, 0, 1, 3)).astype(BF16),
                     kp_new, kmax, k_store.reshape(b, t, QK_WIDTH), v_bf.reshape(ATTN_HEADS, b, t, V_DIM),
                     w["group_ones"], lam_vecs, w["subln"], pos0=past)
    h2d, hn_t, scores_t = _outproj(x2d, pool_bf, att.reshape(n, ATTN_WIDTH), w["w_out"], w["norm_ffn"],
                                   w["w_peer_q"], w["sub_keys"])
    cnt, ea, rank, eb = _topk(scores_t)
    y = _peer(hn_t, h2d, w["peer_u"], w["peer_vt"], cnt, ea, rank, eb)
    tail = jnp.concatenate([pool_prev, p.reshape(b, t, POOL_WIDTH)], axis=1)[:, -POOL_TAIL:]
    return (y.reshape(b, t, D_MODEL), k_store.reshape(b, t, ATTN_HEADS, 2 * QK_DIM),
            v.reshape(b, t, ATTN_HEADS, V_DIM), tail)


def kernel(x_prompt, x_sample, cache_k, cache_v, state_pool, norm_mix, w_in, q_norm, k_norm, lambda_q1, lambda_k1, lambda_q2, lambda_k2, subln, w_pool, pool_scale, w_out, norm_ffn, w_peer_q, peer_sub_keys, peer_u, peer_v):
    depth = w_in.shape[0]
    assert depth == 1
    l = 0
    group = jnp.arange(QK_WIDTH) // QK_DIM
    w = {
        "norm_mix": norm_mix[l][None, :],
        "w_in": w_in[l].astype(BF16),
        "q_gain": jnp.tile(q_norm[l], QK_WIDTH // QK_DIM)[None, :],
        "k_gain": jnp.tile(k_norm[l], QK_WIDTH // QK_DIM)[None, :],
        "k_gain2_max": jnp.full((1, QK_WIDTH), jnp.max(jnp.square(k_norm[l])), F32),
        "group_ones": (group[:, None] == group[None, :]).astype(BF16),
        "w_pool": w_pool[l].astype(BF16),
        "pool_scale": pool_scale[l][None, :],
        "subln": subln[l][None, :],
        "w_out": w_out[l].astype(BF16),
        "norm_ffn": norm_ffn[l][None, :],
        "w_peer_q": w_peer_q[l].astype(BF16),
        "sub_keys": peer_sub_keys[l].reshape(2 * PEER_HEADS, PEER_KEYS, PEER_HALF).astype(BF16),
        "peer_u": peer_u[l].astype(BF16),
        "peer_vt": jnp.transpose(peer_v[l].reshape(PEER_EXPERTS // PEER_BLOCK, PEER_BLOCK, D_MODEL),
                                 (0, 2, 1)).astype(BF16),
    }
    lam_vecs = jnp.stack([lambda_q1[l], lambda_k1[l], lambda_q2[l], lambda_k2[l]]).astype(F32)
    bp = x_prompt.shape[0]
    zero_pool = jnp.zeros((bp, POOL_TAIL, POOL_WIDTH), F32)
    zero_k = jnp.zeros((bp, 0, ATTN_HEADS, 2 * QK_DIM), F32)
    zero_v = jnp.zeros((bp, 0, ATTN_HEADS, V_DIM), F32)
    yp, kp, vp, pp = _stream(x_prompt, zero_pool, zero_k, zero_v, w, lam_vecs)
    ys, kn, vn, pn = _stream(x_sample, state_pool[l], cache_k[l], cache_v[l], w, lam_vecs)
    return (yp, ys, kp[None], vp[None], pp[None], kn[None], vn[None], pn[None])
```

```python
import functools
import math

import jax
import jax.numpy as jnp
from jax import lax
from jax.experimental import pallas as pl
from jax.experimental.pallas import tpu as pltpu

F32 = jnp.float32
BF16 = jnp.bfloat16

D_MODEL = 1024
CHUNK = 64
POOL_WIDTH = 512
POOL_WINDOWS = (2, 4, 8, 16)
POOL_GROUP = 128
POOL_TAIL = 15
POOL_HALO = 16
ATTN_HEADS = 4
QK_DIM = 64
V_DIM = 128
QK_WIDTH = 512
ATTN_WIDTH = 512
IN_WIDTH = 2048
ALIBI_SLOPES = tuple(2.0 ** (-8.0 * (h + 1) / ATTN_HEADS) for h in range(ATTN_HEADS))
PEER_HEADS = 8
PEER_KEYS = 128
PEER_EXPERTS = PEER_KEYS * PEER_KEYS
PEER_HALF = 128
PEER_TOPK = 16
PEER_BLOCK = 1024
NORM_EPS = 1e-6
NEG_INF = -1e30
LAM_INIT = 0.8 - 0.6 * math.exp(-0.3 * 0)
LOG2E = 1.4426950408889634
KEY_BLOCK = 512
POS_PERIOD = 256
MAX_SAFE_BOUND = 60.0

VMEM_LIMIT_BYTES = 56 * 1024 * 1024

_CAND_PIECES = ((0, 1, 16), (1, 1, 8), (2, 1, 5), (3, 1, 4), (4, 1, 3), (5, 1, 2), (6, 1, 2), (7, 1, 2), (8, 8, 1))
_CAND_ROWS = 80


def _params(semantics):
    return pltpu.CompilerParams(dimension_semantics=semantics, vmem_limit_bytes=VMEM_LIMIT_BYTES)


def _rms_rows(x, gain):
    ms = jnp.mean(x * x, axis=-1, keepdims=True)
    return x * lax.rsqrt(ms + NORM_EPS) * gain


def _dot_nt(a, b):
    return lax.dot_general(a, b, (((1,), (1,)), ((), ())), preferred_element_type=F32)


_AUG_START = (QK_DIM, 0)


def _store_aug_keys(kp_ref, k, row0):
    tm = k.shape[0]
    lane = lax.broadcasted_iota(jnp.int32, (tm, V_DIM), 1)
    pos = ((lax.broadcasted_iota(jnp.int32, (tm, V_DIM), 0) + row0) & (POS_PERIOD - 1)).astype(F32)
    for h in range(ATTN_HEADS):
        kh = k[:, h * V_DIM:(h + 1) * V_DIM]
        for sub in range(2):
            a0 = _AUG_START[sub]
            aug = jnp.where((lane >= a0) & (lane < a0 + 3), 1.0,
                            jnp.where((lane >= a0 + 3) & (lane < a0 + 6), pos, 0.0))
            own = (lane < QK_DIM) if sub == 0 else (lane >= QK_DIM)
            kp_ref[0, h, :, sub * V_DIM:(sub + 1) * V_DIM] = jnp.where(own, kh, aug).astype(BF16)


def _running_max(ref, first, tile_max):
    @pl.when(first)
    def _():
        ref[0] = tile_max

    @pl.when(jnp.logical_not(first))
    def _():
        ref[0] = jnp.maximum(ref[0], tile_max)


def _inproj_kernel(x_ref, g_ref, win_ref, qg_ref, kg_ref, kg2_ref, gm_ref, pprev_ref, wpool_ref, pscale_ref,
                   p_ref, ks_ref, v_ref, qb_ref, kp_ref, kmax_ref, vb_ref, pool_ref, ext_ref,
                   *, tm, tiles_per_batch, pos0):
    t_in_batch = lax.rem(pl.program_id(0), tiles_per_batch)
    hn = _rms_rows(x_ref[...], g_ref[...]).astype(BF16)
    proj = jnp.dot(hn, win_ref[...], preferred_element_type=F32)
    p = proj[:, :POOL_WIDTH]
    q = proj[:, POOL_WIDTH:POOL_WIDTH + QK_WIDTH]
    k = proj[:, POOL_WIDTH + QK_WIDTH:POOL_WIDTH + 2 * QK_WIDTH]
    v = proj[:, POOL_WIDTH + 2 * QK_WIDTH:]

    def group_rms(z, gain):
        sq = z * z
        hi = sq.astype(BF16)
        lo = (sq - hi.astype(F32)).astype(BF16)
        ss = (jnp.dot(hi, gm_ref[...], preferred_element_type=F32)
              + jnp.dot(lo, gm_ref[...], preferred_element_type=F32))
        inv = lax.rsqrt(ss * (1.0 / QK_DIM) + NORM_EPS)
        return z * inv * gain, ss * inv * inv

    qn, _ = group_rms(q, qg_ref[...])
    kn, k_unit = group_rms(k, kg_ref[...])
    p_ref[...] = p
    for h in range(ATTN_HEADS):
        ks_ref[pl.ds(h, tm, stride=ATTN_HEADS), :] = kn[:, h * V_DIM:(h + 1) * V_DIM]
        v_ref[pl.ds(h, tm, stride=ATTN_HEADS), :] = v[:, h * V_DIM:(h + 1) * V_DIM]
    qb = (qn * (QK_DIM ** -0.5 * LOG2E)).astype(BF16)
    vb = v.astype(BF16)
    for h in range(ATTN_HEADS):
        qb_ref[h] = qb[:, h * V_DIM:(h + 1) * V_DIM]
        vb_ref[h] = vb[:, h * V_DIM:(h + 1) * V_DIM]
    _store_aug_keys(kp_ref, kn, pos0 + t_in_batch * tm)
    _running_max(kmax_ref, t_in_batch == 0, jnp.max(k_unit * kg2_ref[...], axis=0, keepdims=True))

    @pl.when(t_in_batch == 0)
    def _():
        ext_ref[0:POOL_HALO, :] = pprev_ref[0]

    ext_ref[POOL_HALO:POOL_HALO + tm, :] = p
    pos = lax.broadcasted_iota(jnp.int32, (tm, POOL_GROUP), 0) + (pos0 + t_in_batch * tm)
    mixed = []
    for g, w in enumerate(POOL_WINDOWS):
        lanes = slice(g * POOL_GROUP, (g + 1) * POOL_GROUP)
        win = ext_ref[POOL_HALO:POOL_HALO + tm, lanes]
        for j in range(1, w):
            win = win + ext_ref[POOL_HALO - j:POOL_HALO - j + tm, lanes]
        cnt = jnp.minimum(w, pos + 1).astype(F32)
        pooled = win / cnt - p[:, lanes]
        mixed.append(jnp.dot(pooled.astype(BF16), wpool_ref[g], preferred_element_type=F32))
    pool_ref[...] = (jnp.concatenate(mixed, axis=1) * pscale_ref[...]).astype(BF16)
    ext_ref[0:POOL_HALO, :] = ext_ref[tm:tm + POOL_HALO, :]


def _inproj(x2d, pprev, norm_mix, w_in_bf, qg, kg, kg2, gm, w_pool_bf, pool_scale, *, seq, pos0):
    n = x2d.shape[0]
    b = n // seq
    tm = min(seq, 512)
    tiles_per_batch = seq // tm
    row = lambda i: (i, 0)
    full2 = lambda i: (0, 0)
    rows_spec = pl.BlockSpec((tm, POOL_WIDTH), row)
    out_w = lambda dt: jax.ShapeDtypeStruct((n, POOL_WIDTH), dt)
    out_h = jax.ShapeDtypeStruct((ATTN_HEADS, n, V_DIM), BF16)
    by_head = pl.BlockSpec((ATTN_HEADS, tm, V_DIM), lambda i: (0, i, 0))
    out_cache = jax.ShapeDtypeStruct((n * ATTN_HEADS, V_DIM), F32)
    cache_spec = pl.BlockSpec((tm * ATTN_HEADS, V_DIM), row)
    return pl.pallas_call(
        functools.partial(_inproj_kernel, tm=tm, tiles_per_batch=tiles_per_batch, pos0=pos0),
        grid=(n // tm,),
        in_specs=[
            pl.BlockSpec((tm, D_MODEL), row),
            pl.BlockSpec((1, D_MODEL), full2),
            pl.BlockSpec((D_MODEL, IN_WIDTH), full2),
            pl.BlockSpec((1, QK_WIDTH), full2),
            pl.BlockSpec((1, QK_WIDTH), full2),
            pl.BlockSpec((1, QK_WIDTH), full2),
            pl.BlockSpec((QK_WIDTH, QK_WIDTH), full2),
            pl.BlockSpec((1, POOL_HALO, POOL_WIDTH), lambda i: (i // tiles_per_batch, 0, 0)),
            pl.BlockSpec((len(POOL_WINDOWS), POOL_GROUP, POOL_GROUP), lambda i: (0, 0, 0)),
            pl.BlockSpec((1, POOL_WIDTH), full2),
        ],
        out_specs=[
            rows_spec, cache_spec, cache_spec, by_head,
            pl.BlockSpec((1, ATTN_HEADS, tm, 2 * V_DIM),
                         lambda i: (i // tiles_per_batch, 0, lax.rem(i, tiles_per_batch), 0)),
            pl.BlockSpec((1, 1, QK_WIDTH), lambda i: (i // tiles_per_batch, 0, 0)),
            by_head, rows_spec,
        ],
        out_shape=[out_w(F32), out_cache, out_cache, out_h,
                   jax.ShapeDtypeStruct((b, ATTN_HEADS, seq, 2 * V_DIM), BF16),
                   jax.ShapeDtypeStruct((b, 1, QK_WIDTH), F32), out_h, out_w(BF16)],
        scratch_shapes=[pltpu.VMEM((tm + POOL_HALO, POOL_WIDTH), F32)],
        compiler_params=_params(("arbitrary",)),
        name="inproj_pool",
    )(x2d, norm_mix, w_in_bf, qg, kg, kg2, gm, pprev, w_pool_bf, pool_scale)


def _keyprep_kernel(k_ref, gm_ref, kp_ref, kmax_ref, *, tm):
    t = pl.program_id(1)
    k = k_ref[0].astype(F32)
    ss = jnp.dot((k * k).astype(BF16), gm_ref[...], preferred_element_type=F32)
    _running_max(kmax_ref, t == 0, jnp.max(ss, axis=0, keepdims=True))
    _store_aug_keys(kp_ref, k, t * tm)


def _keyprep(k3d, gm):
    b, s, _ = k3d.shape
    tm = min(s, 512)
    return pl.pallas_call(
        functools.partial(_keyprep_kernel, tm=tm),
        grid=(b, s // tm),
        in_specs=[
            pl.BlockSpec((1, tm, QK_WIDTH), lambda bi, ti: (bi, ti, 0)),
            pl.BlockSpec((QK_WIDTH, QK_WIDTH), lambda bi, ti: (0, 0)),
        ],
        out_specs=[
            pl.BlockSpec((1, ATTN_HEADS, tm, 2 * V_DIM), lambda bi, ti: (bi, 0, ti, 0)),
            pl.BlockSpec((1, 1, QK_WIDTH), lambda bi, ti: (bi, 0, 0)),
        ],
        out_shape=[
            jax.ShapeDtypeStruct((b, ATTN_HEADS, s, 2 * V_DIM), BF16),
            jax.ShapeDtypeStruct((b, 1, QK_WIDTH), F32),
        ],
        compiler_params=_params(("arbitrary", "arbitrary")),
        name="attn_keyprep",
    )(k3d, gm)


def _split3(a):
    hi = a.astype(BF16).astype(F32)
    r = a - hi
    mid = r.astype(BF16).astype(F32)
    return hi, mid, r - mid


def _diff_finish(acc, l, lam_ref, subln_ref, tq):
    lam_v = lam_ref[...]
    lam = (jnp.exp(jnp.sum(lam_v[0:1] * lam_v[1:2], axis=1, keepdims=True))
           - jnp.exp(jnp.sum(lam_v[2:3] * lam_v[3:4], axis=1, keepdims=True)) + LAM_INIT)
    o = acc / l
    o = o[:tq] - lam * o[tq:]
    return (_rms_rows(o, subln_ref[...]) * (1.0 - LAM_INIT)).astype(BF16)


def _attn_fast_kernel(tab_ref, q_ref, kpast_ref, vpast_ref, kdiag_ref, vdiag_ref, kmax_ref, lam_ref, subln_ref,
                      o_ref, bound_ref, acc_ref, l_ref, bdiag_ref, s_ref, *, tq, tk, pos0):
    h = pl.program_id(1)
    qi = pl.program_id(2)
    slope = tab_ref[h, 0]

    @pl.when(qi == 0)
    def _():
        r = lax.rem(lax.broadcasted_iota(jnp.int32, (2 * tq, tq), 0), tq)
        c = lax.broadcasted_iota(jnp.int32, (2 * tq, tq), 1)
        allowed = (c // CHUNK) <= (r // CHUNK)
        fix = (c - (c & (POS_PERIOD - 1)) - 2 * jnp.maximum(c - r, 0)).astype(F32)
        bdiag_ref[...] = jnp.where(allowed, slope * fix, NEG_INF)

    q = q_ref[0, 0].astype(F32)
    lane = lax.broadcasted_iota(jnp.int32, q.shape, 1)
    first = lane < QK_DIM
    qsq = q * q
    km = kmax_ref[0]
    bounds = (jnp.sqrt(jnp.sum(jnp.where(first, qsq, 0.0), axis=1, keepdims=True) * km[:, 0:1]),
              jnp.sqrt(jnp.sum(jnp.where(first, 0.0, qsq), axis=1, keepdims=True) * km[:, QK_DIM:QK_DIM + 1]))
    bound_ref[0, 0, 0] = jnp.broadcast_to(
        jnp.maximum(jnp.max(bounds[0], axis=0, keepdims=True), jnp.max(bounds[1], axis=0, keepdims=True)),
        (8, V_DIM))
    row = lax.broadcasted_iota(jnp.int32, (tq, 1), 0).astype(F32)

    def augmented(sub):
        a0 = _AUG_START[sub]
        pieces = _split3(-(slope * row + bounds[sub])) + (tab_ref[h, 1], tab_ref[h, 2], tab_ref[h, 3])
        aug = jnp.zeros(q.shape, F32)
        for i, piece in enumerate(pieces):
            aug = jnp.where(lane == a0 + i, piece, aug)
        own = first if sub == 0 else jnp.logical_not(first)
        return jnp.where(own, q, aug).astype(BF16)

    acc_ref[...] = jnp.zeros_like(acc_ref)
    l_ref[...] = jnp.zeros_like(l_ref)

    def consume(s, vblk):
        p = jnp.exp2(s)
        width = p.shape[1]
        if width >= V_DIM:
            part = p[:, 0:V_DIM]
            for c0 in range(V_DIM, width, V_DIM):
                part = part + p[:, c0:c0 + V_DIM]
            l_ref[...] += part
        else:
            l_ref[:, 0:width] += p
        acc_ref[...] += jnp.dot(p.astype(BF16), vblk, preferred_element_type=F32)

    base = pos0 + qi * tq
    q_past = (augmented(0), augmented(1))

    n_past = base // tk
    n_wide = n_past // 2
    col = lax.broadcasted_iota(jnp.int32, (1, 2 * tk), 1)
    col_offset = slope * (col - (col & (POS_PERIOD - 1))).astype(F32)

    def past_exponents(start, width):
        start = start if isinstance(start, int) else pl.multiple_of(start, tk)
        s = jnp.concatenate([_dot_nt(q_past[0], kpast_ref[0, 0, pl.ds(start, width), 0:V_DIM]),
                             _dot_nt(q_past[1], kpast_ref[0, 0, pl.ds(start, width), V_DIM:2 * V_DIM])], axis=0)
        return s + (col_offset[:, 0:width] - slope * (base - start).astype(F32))

    def past_values(start, width):
        start = start if isinstance(start, int) else pl.multiple_of(start, tk)
        return vpast_ref[0, 0, pl.ds(start, width), :]

    @pl.when(n_wide > 0)
    def _():
        s_ref[...] = past_exponents(0, 2 * tk)

    def past_block(j, _):
        s_next = past_exponents((j + 1) * (2 * tk), 2 * tk)
        consume(s_ref[...], past_values(j * (2 * tk), 2 * tk))
        s_ref[...] = s_next
        return 0

    lax.fori_loop(0, n_wide - 1, past_block, 0)

    @pl.when(n_wide > 0)
    def _():
        consume(s_ref[...], past_values((n_wide - 1) * (2 * tk), 2 * tk))

    @pl.when(n_past > 2 * n_wide)
    def _():
        start = (n_past - 1) * tk
        consume(past_exponents(start, tk), past_values(start, tk))

    s = jnp.concatenate([_dot_nt(q_past[0], kdiag_ref[0, 0, :, 0:V_DIM]),
                         _dot_nt(q_past[1], kdiag_ref[0, 0, :, V_DIM:2 * V_DIM])], axis=0)
    consume(s + bdiag_ref[...], vdiag_ref[0, 0])
    l = jnp.sum(l_ref[...], axis=1, keepdims=True)
    o_ref[0] = _diff_finish(acc_ref[...], l, lam_ref, subln_ref, tq)


def _attention_fast(tab, q_hm, kp_past, v_past, kp_new, v_new, kmax, lam_vecs, subln, *, pos0):
    _, b, t, _ = q_hm.shape
    s_past = kp_past.shape[2]
    tq = min(t, KEY_BLOCK)
    tk = KEY_BLOCK
    nq = t // tq
    assert pos0 % tk == 0 and t % tq == 0 and (tq == tk or nq == 1) and pos0 % CHUNK == 0 and tq % 8 == 0
    assert s_past >= pos0 + t - tq
    const2 = lambda bi, hi, qi: (0, 0)
    return pl.pallas_call(
        functools.partial(_attn_fast_kernel, tq=tq, tk=tk, pos0=pos0),
        grid=(b, ATTN_HEADS, nq),
        in_specs=[
            pl.BlockSpec(memory_space=pltpu.SMEM),
            pl.BlockSpec((1, 1, tq, V_DIM), lambda bi, hi, qi: (hi, bi, qi, 0)),
            pl.BlockSpec((1, 1, s_past, 2 * V_DIM), lambda bi, hi, qi: (bi, hi, 0, 0)),
            pl.BlockSpec((1, 1, s_past, V_DIM), lambda bi, hi, qi: (hi, bi, 0, 0)),
            pl.BlockSpec((1, 1, tq, 2 * V_DIM), lambda bi, hi, qi: (bi, hi, qi, 0)),
            pl.BlockSpec((1, 1, tq, V_DIM), lambda bi, hi, qi: (hi, bi, qi, 0)),
            pl.BlockSpec((1, 1, V_DIM), lambda bi, hi, qi: (bi, 0, hi)),
            pl.BlockSpec((4, QK_DIM), const2),
            pl.BlockSpec((1, V_DIM), const2),
        ],
        out_specs=[
            pl.BlockSpec((1, tq, V_DIM), lambda bi, hi, qi: (bi, qi, hi)),
            pl.BlockSpec((1, 1, 1, 8, V_DIM), lambda bi, hi, qi: (bi, hi, qi, 0, 0)),
        ],
        out_shape=[
            jax.ShapeDtypeStruct((b, t, ATTN_WIDTH), BF16),
            jax.ShapeDtypeStruct((b, ATTN_HEADS, nq, 8, V_DIM), F32),
        ],
        scratch_shapes=[pltpu.VMEM((2 * tq, V_DIM), F32), pltpu.VMEM((2 * tq, V_DIM), F32),
                        pltpu.VMEM((2 * tq, tq), F32), pltpu.VMEM((2 * tq, 2 * tk), F32)],
        compiler_params=_params(("arbitrary", "arbitrary", "arbitrary")),
        name="diff_attention",
    )(tab, q_hm, kp_past, v_past, kp_new, v_new, kmax, lam_vecs, subln)


def _attn_online_kernel(tab_ref, q_ref, k_ref, v_ref, lam_ref, subln_ref, o_ref, boff_ref, bdiag_ref,
                        *, tq, tk, pos0):
    h = pl.program_id(1)
    qi = pl.program_id(2)
    slope = tab_ref[h, 0]

    @pl.when(qi == 0)
    def _():
        r = lax.rem(lax.broadcasted_iota(jnp.int32, (2 * tq, tk), 0), tq)
        c = lax.broadcasted_iota(jnp.int32, (2 * tq, tk), 1)
        boff_ref[...] = -slope * (r - c).astype(F32)
        r = lax.rem(lax.broadcasted_iota(jnp.int32, (2 * tq, tq), 0), tq)
        c = lax.broadcasted_iota(jnp.int32, (2 * tq, tq), 1)
        allowed = (c // CHUNK) <= (r // CHUNK)
        bdiag_ref[...] = jnp.where(allowed, -slope * jnp.abs(r - c).astype(F32), NEG_INF)

    q = q_ref[0]
    lane = lax.broadcasted_iota(jnp.int32, q.shape, 1)
    zero = jnp.zeros_like(q)
    qs = jnp.concatenate([jnp.where(lane < QK_DIM, q, zero), jnp.where(lane >= QK_DIM, q, zero)], axis=0)
    base = pos0 + qi * tq

    def update(s, vblk, carry):
        m, l, acc = carry
        m_new = jnp.maximum(m, jnp.max(s, axis=1, keepdims=True))
        alpha = jnp.exp2(m - m_new)
        p = jnp.exp2(s - m_new)
        l = alpha * l + jnp.sum(p, axis=1, keepdims=True)
        acc = alpha * acc + jnp.dot(p.astype(BF16), vblk, preferred_element_type=F32)
        return m_new, l, acc

    def past_block(kb, carry):
        start = pl.multiple_of(kb * tk, tk)
        s = _dot_nt(qs, k_ref[0, pl.ds(start, tk), :])
        s = s + boff_ref[...] - slope * (base - kb * tk).astype(F32)
        return update(s, v_ref[0, pl.ds(start, tk), :], carry)

    carry = (jnp.full((2 * tq, 1), NEG_INF, F32), jnp.zeros((2 * tq, 1), F32), jnp.zeros((2 * tq, V_DIM), F32))
    carry = lax.fori_loop(0, base // tk, past_block, carry)
    dstart = pl.multiple_of(base, tq)
    s = _dot_nt(qs, k_ref[0, pl.ds(dstart, tq), :]) + bdiag_ref[...]
    _, l, acc = update(s, v_ref[0, pl.ds(dstart, tq), :], carry)
    o_ref[0] = _diff_finish(acc, l, lam_ref, subln_ref, tq)


def _attention_online(tab, q_bf, k_all, v_all, lam_vecs, subln, *, pos0):
    b, t, _ = q_bf.shape
    s = k_all.shape[1]
    tq = min(t, KEY_BLOCK)
    tk = KEY_BLOCK
    assert pos0 % tk == 0 and t % tq == 0 and (tq == tk or t == tq) and pos0 % CHUNK == 0 and tq % 8 == 0
    assert s == pos0 + t
    return pl.pallas_call(
        functools.partial(_attn_online_kernel, tq=tq, tk=tk, pos0=pos0),
        grid=(b, ATTN_HEADS, t // tq),
        in_specs=[
            pl.BlockSpec(memory_space=pltpu.SMEM),
            pl.BlockSpec((1, tq, V_DIM), lambda bi, hi, qi: (bi, qi, hi)),
            pl.BlockSpec((1, s, V_DIM), lambda bi, hi, qi: (bi, 0, hi)),
            pl.BlockSpec((1, s, V_DIM), lambda bi, hi, qi: (bi, 0, hi)),
            pl.BlockSpec((4, QK_DIM), lambda bi, hi, qi: (0, 0)),
            pl.BlockSpec((1, V_DIM), lambda bi, hi, qi: (0, 0)),
        ],
        out_specs=pl.BlockSpec((1, tq, V_DIM), lambda bi, hi, qi: (bi, qi, hi)),
        out_shape=jax.ShapeDtypeStruct((b, t, ATTN_WIDTH), BF16),
        scratch_shapes=[pltpu.VMEM((2 * tq, tk), F32), pltpu.VMEM((2 * tq, tq), F32)],
        compiler_params=_params(("arbitrary", "arbitrary", "arbitrary")),
        name="diff_attention_online",
    )(tab, q_bf, k_all, v_all, lam_vecs, subln)


def _alibi_table():
    cs = jnp.asarray([LOG2E * s for s in ALIBI_SLOPES], F32)
    return jnp.stack((cs,) + _split3(cs), axis=1)


def _attention(q_hm, k_prev_bf, v_prev_hm, kp_new, kmax, k_new, v_hm, gm, lam_vecs, subln, *, pos0):
    tab = _alibi_table()
    if pos0:
        kp_past, kmax_past = _keyprep(k_prev_bf, gm)
        kmax = jnp.maximum(kmax, kmax_past)
        v_past = v_prev_hm
    else:
        kp_past, v_past = kp_new, v_hm
    att, bound = _attention_fast(tab, q_hm, kp_past, v_past, kp_new, v_hm, kmax, lam_vecs, subln, pos0=pos0)

    def online():
        rows = lambda x: jnp.transpose(x, (1, 2, 0, 3)).reshape(x.shape[1], x.shape[2], ATTN_WIDTH)
        k_bf = k_new.astype(BF16)
        k_all = jnp.concatenate([k_prev_bf, k_bf], axis=1) if pos0 else k_bf
        v_all = jnp.concatenate([rows(v_prev_hm), rows(v_hm)], axis=1) if pos0 else rows(v_hm)
        return _attention_online(tab, rows(q_hm), k_all, v_all, lam_vecs, subln, pos0=pos0)

    return lax.cond(jnp.max(bound) <= MAX_SAFE_BOUND, lambda: att, online)


def _outproj_kernel(x_ref, pool_ref, att_ref, wout_ref, g_ref, wq_ref, sk_ref, h_ref, hnt_ref, st_ref):
    mix = (jnp.dot(pool_ref[...], wout_ref[0:POOL_WIDTH, :], preferred_element_type=F32)
           + jnp.dot(att_ref[...], wout_ref[POOL_WIDTH:, :], preferred_element_type=F32))
    h = x_ref[...] + mix
    h_ref[...] = h
    hn = _rms_rows(h, g_ref[...])
    hnt_ref[...] = hn.T.astype(BF16)
    q = jnp.dot(hn.astype(BF16), wq_ref[...], preferred_element_type=F32)
    for g in range(2 * PEER_HEADS):
        qg = q[:, g * PEER_HALF:(g + 1) * PEER_HALF].astype(BF16)
        st_ref[g] = _dot_nt(sk_ref[g], qg)


def _outproj(x2d, pool_bf, att_bf, w_out_bf, norm_ffn, w_q_bf, sub_keys_bf):
    n = x2d.shape[0]
    tm = min(n, 256)
    row = lambda i: (i, 0)
    full2 = lambda i: (0, 0)
    return pl.pallas_call(
        _outproj_kernel,
        grid=(n // tm,),
        in_specs=[
            pl.BlockSpec((tm, D_MODEL), row),
            pl.BlockSpec((tm, POOL_WIDTH), row),
            pl.BlockSpec((tm, ATTN_WIDTH), row),
            pl.BlockSpec((D_MODEL, D_MODEL), full2),
            pl.BlockSpec((1, D_MODEL), full2),
            pl.BlockSpec((D_MODEL, 2 * PEER_HEADS * PEER_HALF), full2),
            pl.BlockSpec((2 * PEER_HEADS, PEER_KEYS, PEER_HALF), lambda i: (0, 0, 0)),
        ],
        out_specs=[
            pl.BlockSpec((tm, D_MODEL), row),
            pl.BlockSpec((D_MODEL, tm), lambda i: (0, i)),
            pl.BlockSpec((2 * PEER_HEADS, PEER_KEYS, tm), lambda i: (0, 0, i)),
        ],
        out_shape=[
            jax.ShapeDtypeStruct((n, D_MODEL), F32),
            jax.ShapeDtypeStruct((D_MODEL, n), BF16),
            jax.ShapeDtypeStruct((2 * PEER_HEADS, PEER_KEYS, n), F32),
        ],
        compiler_params=_params(("arbitrary",)),
        name="outproj_peer_query",
    )(x2d, pool_bf, att_bf, w_out_bf, norm_ffn, w_q_bf, sub_keys_bf)


def _take_max(cur, iota, exact):
    m = jnp.max(cur, axis=0, keepdims=True)
    if exact:
        idx = jnp.min(jnp.where(cur == m, iota, cur.shape[0]), axis=0, keepdims=True)
        return m, iota == idx
    return m, cur == m


def _top16_rows(s, iota_k, exact):
    tn = s.shape[1]
    row16 = lax.broadcasted_iota(jnp.int32, (PEER_TOPK, tn), 0)
    rank = jnp.full(s.shape, float(PEER_TOPK), F32)
    vals = jnp.zeros((PEER_TOPK, tn), F32)
    cur = s
    for i in range(PEER_TOPK):
        m, hit = _take_max(cur, iota_k, exact)
        rank = jnp.where(hit, float(i), rank)
        cur = jnp.where(hit, NEG_INF, cur)
        vals = jnp.where(row16 == i, m, vals)
    return vals, rank


_RANK_SUM = float(sum(range(PEER_TOPK)) + PEER_TOPK * (PEER_KEYS - PEER_TOPK))


def _topk_kernel(s_ref, cnt_ref, ea_ref, rank_ref, eb_ref, *, tn):
    surplus = _topk_heads(s_ref, cnt_ref, ea_ref, rank_ref, eb_ref, tn, exact=False)

    @pl.when(jnp.max(surplus) > 0.0)
    def _():
        _topk_heads(s_ref, cnt_ref, ea_ref, rank_ref, eb_ref, tn, exact=True)


def _topk_heads(s_ref, cnt_ref, ea_ref, rank_ref, eb_ref, tn, exact):
    iota_k = lax.broadcasted_iota(jnp.int32, (PEER_KEYS, tn), 0)
    iota_c = lax.broadcasted_iota(jnp.int32, (_CAND_ROWS, tn), 0)
    row8 = lax.broadcasted_iota(jnp.int32, (8, tn), 0)
    surplus = jnp.zeros((1, tn), F32)
    for h in range(PEER_HEADS):
        sa = s_ref[2 * h]
        sb = s_ref[2 * h + 1]
        va, rank_a = _top16_rows(sa, iota_k, exact)
        vb, rank_b = _top16_rows(sb, iota_k, exact)
        pieces = []
        for i0, ni, nj in _CAND_PIECES:
            if ni == 1:
                rows = 16 if nj > 8 else 8
                piece = va[i0:i0 + 1] + vb[0:rows]
                if nj < rows:
                    piece = jnp.where(row8 < nj, piece, NEG_INF)
            else:
                piece = va[i0:i0 + ni] + vb[0:1]
            pieces.append(piece)
        cand = jnp.concatenate(pieces, axis=0)
        cur = cand
        sel = jnp.zeros(cand.shape, F32)
        for _ in range(PEER_TOPK):
            _, hit = _take_max(cur, iota_c, exact)
            sel = jnp.where(hit, 1.0, sel)
            cur = jnp.where(hit, NEG_INF, cur)
        taken = (jnp.sum(rank_a, axis=0, keepdims=True) + jnp.sum(rank_b, axis=0, keepdims=True)
                 - jnp.sum(sel, axis=0, keepdims=True))
        surplus = jnp.maximum(surplus, jnp.abs(taken - (2.0 * _RANK_SUM - PEER_TOPK)))
        z = jnp.sum(jnp.where(sel > 0.0, jnp.exp(cand - cand[0:1]), 0.0), axis=0, keepdims=True)
        cnt = jnp.zeros(sa.shape, F32)
        off = 0
        for i0, ni, nj in _CAND_PIECES:
            rows = (16 if nj > 8 else 8) if ni == 1 else ni
            if ni == 1:
                c_i = jnp.sum(sel[off:off + rows], axis=0, keepdims=True)
                cnt = jnp.where(rank_a == float(i0), c_i, cnt)
            else:
                for r in range(ni):
                    cnt = jnp.where(rank_a == float(i0 + r), sel[off + r:off + r + 1], cnt)
            off += rows
        cnt_ref[h] = cnt
        ea_ref[h] = jnp.exp(sa - va[0:1])
        rank_ref[h] = rank_b.astype(BF16)
        eb_ref[h] = (jnp.exp(sb - vb[0:1]) / z).astype(BF16)
    return surplus


def _topk(scores_t):
    n = scores_t.shape[2]
    tn = min(n, 128)
    spec_in = pl.BlockSpec((2 * PEER_HEADS, PEER_KEYS, tn), lambda i: (0, 0, i))
    spec_out = pl.BlockSpec((PEER_HEADS, PEER_KEYS, tn), lambda i: (0, 0, i))
    shape_out = lambda dt: jax.ShapeDtypeStruct((PEER_HEADS, PEER_KEYS, n), dt)
    return pl.pallas_call(
        functools.partial(_topk_kernel, tn=tn),
        grid=(n // tn,),
        in_specs=[spec_in],
        out_specs=[spec_out] * 4,
        out_shape=[shape_out(F32), shape_out(F32), shape_out(BF16), shape_out(BF16)],
        compiler_params=_params(("arbitrary",)),
        name="peer_topk",
    )(scores_t)


def _peer_kernel(hnt_ref, h_ref, u_ref, vta_ref, vtb_ref, cnt_ref, ea_ref, rank_ref, eb_ref, y_ref,
                 acc_ref, acta_ref, actb_ref, *, te):
    step = pl.program_id(1)
    last = pl.num_programs(1) - 1
    hnt = hnt_ref[...]
    tn = hnt.shape[1]
    pack = 16
    tiles = PEER_KEYS // pack
    zero = jnp.zeros((tiles, pack, tn), BF16)

    def key_row(ref, h, a_glob):
        return jnp.broadcast_to(ref[h, pl.ds(a_glob, 1), :], (pack, tn)).astype(BF16)[None]

    def contribution(act_ref, vt_ref, block):
        coefs = []
        for r0 in range(0, te, PEER_KEYS):
            a_glob = block * (te // PEER_KEYS) + r0 // PEER_KEYS
            gate = None
            for h in range(PEER_HEADS):
                cnt = key_row(cnt_ref, h, a_glob)
                rank = rank_ref[h].reshape(tiles, pack, tn)
                term = jnp.where(rank < cnt, eb_ref[h].reshape(tiles, pack, tn), zero) * key_row(ea_ref, h, a_glob)
                gate = term if gate is None else gate + term
            x = act_ref[r0:r0 + PEER_KEYS, :]
            gelu = 0.5 * x * (1.0 + lax.erf(x * (2.0 ** -0.5)))
            coefs.append(gate.reshape(PEER_KEYS, tn) * gelu.astype(BF16))
        half = te // 2
        per_half = half // PEER_KEYS
        return (jnp.dot(vt_ref[0, :, 0:half], jnp.concatenate(coefs[:per_half], axis=0),
                        preferred_element_type=F32)
                + jnp.dot(vt_ref[0, :, half:te], jnp.concatenate(coefs[per_half:], axis=0),
                          preferred_element_type=F32))

    def produce(act_ref, half):
        act_ref[...] = jnp.dot(u_ref[half * te:(half + 1) * te, :], hnt, preferred_element_type=F32)

    @pl.when(step == 0)
    def _():
        produce(acta_ref, 0)
        produce(actb_ref, 1)
        acc_ref[...] = contribution(acta_ref, vta_ref, 0)

    @pl.when(jnp.logical_and(step > 0, step < last))
    def _():
        produce(acta_ref, 0)
        part_b = contribution(actb_ref, vtb_ref, 2 * step - 1)
        produce(actb_ref, 1)
        acc_ref[...] += part_b + contribution(acta_ref, vta_ref, 2 * step)

    @pl.when(step == last)
    def _():
        total = acc_ref[...] + contribution(actb_ref, vtb_ref, 2 * step - 1)
        y_ref[...] = h_ref[...] + total.T


def _peer(hn_t, h2d, u_bf, vt_bf, cnt, ea, rank, eb):
    n = h2d.shape[0]
    tn = min(n, 512)
    te = PEER_BLOCK
    n_pairs = PEER_EXPERTS // (2 * te)
    tok = lambda i, s: (0, 0, i)
    fac = pl.BlockSpec((PEER_HEADS, PEER_KEYS, tn), tok)
    return pl.pallas_call(
        functools.partial(_peer_kernel, te=te),
        grid=(n // tn, n_pairs + 1),
        in_specs=[
            pl.BlockSpec((D_MODEL, tn), lambda i, s: (0, i)),
            pl.BlockSpec((tn, D_MODEL), lambda i, s: (i, 0)),
            pl.BlockSpec((2 * te, D_MODEL), lambda i, s: (jnp.minimum(s, n_pairs - 1), 0)),
            pl.BlockSpec((1, D_MODEL, te), lambda i, s: (jnp.minimum(2 * s, 2 * n_pairs - 1), 0, 0)),
            pl.BlockSpec((1, D_MODEL, te), lambda i, s: (jnp.maximum(2 * s - 1, 0), 0, 0)),
            fac, fac, fac, fac,
        ],
        out_specs=pl.BlockSpec((tn, D_MODEL), lambda i, s: (i, 0)),
        out_shape=jax.ShapeDtypeStruct((n, D_MODEL), F32),
        scratch_shapes=[pltpu.VMEM((D_MODEL, tn), F32), pltpu.VMEM((te, tn), F32), pltpu.VMEM((te, tn), F32)],
        compiler_params=_params(("arbitrary", "arbitrary")),
        name="peer_dense",
    )(hn_t, h2d, u_bf, vt_bf, vt_bf, cnt, ea, rank, eb)


def _stream(x, pool_prev, k_prev, v_prev, w, lam_vecs):
    b, t, _ = x.shape
    past = k_prev.shape[1]
    n = b * t
    x2d = x.reshape(n, D_MODEL)
    pprev = jnp.concatenate([jnp.zeros((b, POOL_HALO - POOL_TAIL, POOL_WIDTH), F32), pool_prev], axis=1)
    p, k_store, v, q_bf, kp_new, kmax, v_bf, pool_bf = _inproj(
        x2d, pprev, w["norm_mix"], w["w_in"], w["q_gain"], w["k_gain"], w["k_gain2_max"], w["group_ones"],
        w["w_pool"], w["pool_scale"], seq=t, pos0=past)
    att = _attention(q_bf.reshape(ATTN_HEADS, b, t, V_DIM),
                     k_prev.reshape(b, past, QK_WIDTH).astype(BF16),
                     jnp.transpose(v_prev, (2, 0, 1, 3)).astype(BF16),
                     kp_new, kmax, k_store.reshape(b, t, QK_WIDTH), v_bf.reshape(ATTN_HEADS, b, t, V_DIM),
                     w["group_ones"], lam_vecs, w["subln"], pos0=past)
    h2d, hn_t, scores_t = _outproj(x2d, pool_bf, att.reshape(n, ATTN_WIDTH), w["w_out"], w["norm_ffn"],
                                   w["w_peer_q"], w["sub_keys"])
    cnt, ea, rank, eb = _topk(scores_t)
    y = _peer(hn_t, h2d, w["peer_u"], w["peer_vt"], cnt, ea, rank, eb)
    tail = jnp.concatenate([pool_prev, p.reshape(b, t, POOL_WIDTH)], axis=1)[:, -POOL_TAIL:]
    return (y.reshape(b, t, D_MODEL), k_store.reshape(b, t, ATTN_HEADS, 2 * QK_DIM),
            v.reshape(b, t, ATTN_HEADS, V_DIM), tail)


def kernel(x_prompt, x_sample, cache_k, cache_v, state_pool, norm_mix, w_in, q_norm, k_norm, lambda_q1, lambda_k1, lambda_q2, lambda_k2, subln, w_pool, pool_scale, w_out, norm_ffn, w_peer_q, peer_sub_keys, peer_u, peer_v):
    depth = w_in.shape[0]
    assert depth == 1
    l = 0
    group = jnp.arange(QK_WIDTH) // QK_DIM
    w = {
        "norm_mix": norm_mix[l][None, :],
        "w_in": w_in[l].astype(BF16),
        "q_gain": jnp.tile(q_norm[l], QK_WIDTH // QK_DIM)[None, :],
        "k_gain": jnp.tile(k_norm[l], QK_WIDTH // QK_DIM)[None, :],
        "k_gain2_max": jnp.full((1, QK_WIDTH), jnp.max(jnp.square(k_norm[l])), F32),
        "group_ones": (group[:, None] == group[None, :]).astype(BF16),
        "w_pool": w_pool[l].astype(BF16),
        "pool_scale": pool_scale[l][None, :],
        "subln": subln[l][None, :],
        "w_out": w_out[l].astype(BF16),
        "norm_ffn": norm_ffn[l][None, :],
        "w_peer_q": w_peer_q[l].astype(BF16),
        "sub_keys": peer_sub_keys[l].reshape(2 * PEER_HEADS, PEER_KEYS, PEER_HALF).astype(BF16),
        "peer_u": peer_u[l].astype(BF16),
        "peer_vt": jnp.transpose(peer_v[l].reshape(PEER_EXPERTS // PEER_BLOCK, PEER_BLOCK, D_MODEL),
                                 (0, 2, 1)).astype(BF16),
    }
    lam_vecs = jnp.stack([lambda_q1[l], lambda_k1[l], lambda_q2[l], lambda_k2[l]]).astype(F32)
    bp = x_prompt.shape[0]
    zero_pool = jnp.zeros((bp, POOL_TAIL, POOL_WIDTH), F32)
    zero_k = jnp.zeros((bp, 0, ATTN_HEADS, 2 * QK_DIM), F32)
    zero_v = jnp.zeros((bp, 0, ATTN_HEADS, V_DIM), F32)
    yp, kp, vp, pp = _stream(x_prompt, zero_pool, zero_k, zero_v, w, lam_vecs)
    ys, kn, vn, pn = _stream(x_sample, state_pool[l], cache_k[l], cache_v[l], w, lam_vecs)
    return (yp, ys, kp[None], vp[None], pp[None], kn[None], vn[None], pn[None])
```
